```python
import functools
import jax, jax.numpy as jnp
from jax import lax
import numpy as np

D_MODEL = 1024
BATCH = 4
SEQ = 4096
DEPTH = 2
DEC_BATCH = 128
DEC_SEQ = 1
PAST_LEN = 2048
PAGE_SIZE = 128

N_PAGES = PAST_LEN // PAGE_SIZE
HEAD_DIM = 64
W_A = D_MODEL // 2
W_B = D_MODEL - W_A
H_A = W_A // HEAD_DIM
H_B = W_B // HEAD_DIM
H_IDX = 4
D_IDX = 64
TOPK_MAX = 256
D_FF = 4 * D_MODEL
ROPE_THETA = 10000.0
EPS = 1e-6
Q_BLOCK = 128
RET_CHUNK = 128
COL_SIZES = (W_A, W_A, W_A, H_IDX * D_IDX, D_IDX, H_IDX, W_B, W_B, W_B, W_B)
N_IN = sum(COL_SIZES)

kernel_name = "hymba_dsa_retention_decoder_step"

F32 = jnp.float32


def rmsnorm(x, g):
    xf = x.astype(F32)
    y = xf * lax.rsqrt(jnp.mean(xf * xf, axis=-1, keepdims=True) + EPS)
    return (y * g.astype(F32)).astype(x.dtype)


def rope(x, pos):
    half = x.shape[-1] // 2
    inv = ROPE_THETA ** (-jnp.arange(half, dtype=F32) / half)
    ang = pos[:, None] * inv[None, :]
    c = jnp.cos(ang)[None, :, None, :]
    s = jnp.sin(ang)[None, :, None, :]
    xf = x.astype(F32)
    x1, x2 = xf[..., :half], xf[..., half:]
    return jnp.concatenate([x1 * c - x2 * s, x1 * s + x2 * c], axis=-1).astype(x.dtype)


def project_mixers(h, w_in_l, pos):
    B, T, _ = h.shape
    z = h @ w_in_l
    splits = np.cumsum(COL_SIZES)[:-1].tolist()
    qa, ka, va, qi, ki, wi, qb, kb, vb, gb = jnp.split(z, splits, axis=-1)
    qa = rope(qa.reshape(B, T, H_A, HEAD_DIM), pos)
    ka = rope(ka.reshape(B, T, H_A, HEAD_DIM), pos)
    va = va.reshape(B, T, H_A, HEAD_DIM)
    qi = rope(qi.reshape(B, T, H_IDX, D_IDX), pos)
    ki = rope(ki.reshape(B, T, 1, D_IDX), pos)[:, :, 0]
    wi = wi * (H_IDX ** -0.5 * D_IDX ** -0.5)
    qb = rope(qb.reshape(B, T, H_B, HEAD_DIM), pos)
    kb = rope(kb.reshape(B, T, H_B, HEAD_DIM), pos) * (HEAD_DIM ** -0.5)
    vb = vb.reshape(B, T, H_B, HEAD_DIM)
    return qa, ka, va, qi, ki, wi, qb, kb, vb, gb


def indexer_scores(qi, ki, wi):
    dots = jnp.einsum('bthd,bsd->bths', qi.astype(F32), ki.astype(F32))
    return jnp.einsum('bth,bths->bts', wi.astype(F32), jax.nn.relu(dots))


def sparse_attend(q, kg, vg, valid):
    s = jnp.einsum('bthd,btkhd->bthk', q.astype(F32), kg.astype(F32)) * (HEAD_DIM ** -0.5)
    s = jnp.where(valid[:, :, None, :], s, -jnp.inf)
    p = jax.nn.softmax(s, axis=-1)
    return jnp.einsum('bthk,btkhd->bthd', p, vg.astype(F32)).astype(q.dtype)


def gather_rows(src, idx):
    return jax.vmap(lambda s_, i_: s_[i_])(src, idx)


def dsa_prompt(qa, ka, va, qi, ki, wi):
    B, T = qa.shape[:2]
    topk = min(TOPK_MAX, T // 4)
    nb = T // Q_BLOCK
    key_pos = jnp.arange(T)

    def blocks(a):
        return a.reshape((B, nb, Q_BLOCK) + a.shape[2:]).swapaxes(0, 1)

    def one_block(args):
        q_blk, qi_blk, wi_blk, t0 = args
        tq = t0 + jnp.arange(Q_BLOCK)
        sc = indexer_scores(qi_blk, ki, wi_blk)
        sc = jnp.where((key_pos[None, :] <= tq[:, None])[None], sc, -jnp.inf)
        _, sel = lax.top_k(sc, topk)
        valid = sel <= tq[None, :, None]
        return sparse_attend(q_blk, gather_rows(ka, sel), gather_rows(va, sel), valid)

    out = lax.map(one_block, (blocks(qa), blocks(qi), blocks(wi), jnp.arange(nb) * Q_BLOCK))
    return out.swapaxes(0, 1).reshape(B, T, H_A, HEAD_DIM)


def dsa_sample(qa, ka, va, qi, ki, wi, pool_k, pool_v, pool_ki, page_table):
    B, T = qa.shape[:2]
    past = page_table.shape[1] * PAGE_SIZE
    L = past + T
    topk = min(TOPK_MAX, L // 4)
    phys = (page_table[:, :, None] * PAGE_SIZE + jnp.arange(PAGE_SIZE)[None, None, :]).reshape(B, past)
    flat_k = pool_k.reshape(-1, H_A, HEAD_DIM)
    flat_v = pool_v.reshape(-1, H_A, HEAD_DIM)
    flat_ki = pool_ki.reshape(-1, D_IDX)
    ki_all = jnp.concatenate([flat_ki[phys], ki], axis=1)
    tq = past + jnp.arange(T)
    sc = indexer_scores(qi, ki_all, wi)
    sc = jnp.where((jnp.arange(L)[None, :] <= tq[:, None])[None], sc, -jnp.inf)
    _, sel = lax.top_k(sc, topk)
    valid = sel <= tq[None, :, None]
    in_past = (sel < past)[..., None, None]
    rows = jnp.take_along_axis(phys, jnp.clip(sel, 0, past - 1).reshape(B, -1), axis=1).reshape(sel.shape)
    new_idx = jnp.clip(sel - past, 0, T - 1)
    kg = jnp.where(in_past, flat_k[rows], gather_rows(ka, new_idx))
    vg = jnp.where(in_past, flat_v[rows], gather_rows(va, new_idx))
    return sparse_attend(qa, kg, vg, valid)


def retention(q, k, v, s0):
    B, L, H, D = q.shape
    C = RET_CHUNK if L % RET_CHUNK == 0 else L
    n = L // C
    lg = jnp.log1p(-jnp.exp2(-5.0 - jnp.arange(H, dtype=F32)))
    i = jnp.arange(C, dtype=F32)
    diff = i[:, None] - i[None, :]
    dmask = jnp.where(diff[None] >= 0, jnp.exp(jnp.maximum(diff, 0.0)[None] * lg[:, None, None]), 0.0)
    q_dec = jnp.exp((i[:, None] + 1.0) * lg[None, :])
    k_dec = jnp.exp((C - 1.0 - i)[:, None] * lg[None, :])
    c_dec = jnp.exp(C * lg)

    def chunks(a):
        return a.astype(F32).reshape(B, n, C, H, D).swapaxes(0, 1)

    def step(S, qkv):
        qc, kc, vc = qkv
        att = jnp.einsum('bihd,bjhd->bhij', qc, kc) * dmask[None]
        o = (jnp.einsum('bhij,bjhe->bihe', att, vc)
             + jnp.einsum('bihd,bhde->bihe', qc, S) * q_dec[None, :, :, None])
        S = S * c_dec[None, :, None, None] + jnp.einsum('bjhd,bjhe->bhde', kc * k_dec[None, :, :, None], vc)
        return S, o

    S, o = lax.scan(step, s0.astype(F32), (chunks(q), chunks(k), chunks(v)))
    return o.swapaxes(0, 1).reshape(B, L, H, D), S


def retention_mixer(qb, kb, vb, gb, s0, g_ret_l):
    o, S = retention(qb, kb, vb, s0)
    mu = jnp.mean(o, axis=-1, keepdims=True)
    var = jnp.mean(jnp.square(o - mu), axis=-1, keepdims=True)
    o = (o - mu) * lax.rsqrt(var + EPS)
    B, T = o.shape[:2]
    o = o.reshape(B, T, W_B) * g_ret_l.astype(F32)
    return (o * jax.nn.silu(gb.astype(F32))).astype(gb.dtype), S


def trunk_layer(x, l, pos, mixer_a, s0, w_in, w_out, g_ret, g_mix_pre, g_mix_post,
                g_ffn_pre, g_ffn_post, w_up, w_down):
    B, T = x.shape[:2]
    h = rmsnorm(x, g_mix_pre[l])
    qa, ka, va, qi, ki, wi, qb, kb, vb, gb = project_mixers(h, w_in[l], pos)
    a = mixer_a(qa, ka, va, qi, ki, wi)
    b, S = retention_mixer(qb, kb, vb, gb, s0, g_ret[l])
    m = jnp.concatenate([a.reshape(B, T, W_A), b], axis=-1) @ w_out[l]
    x = x + rmsnorm(m, g_mix_post[l])
    f = jnp.square(jax.nn.relu(rmsnorm(x, g_ffn_pre[l]) @ w_up[l])) @ w_down[l]
    x = x + rmsnorm(f, g_ffn_post[l])
    return x, ka, va, ki, S.astype(s0.dtype)


def setup_inputs(seed: int = 0) -> dict:
    key = jax.random.key(seed)
    ks = jax.random.split(key, 20)
    used = DEC_BATCH * N_PAGES
    n_pool = used + max(1, used // 4)
    nrm = jax.random.normal

    def gain(k, shape):
        return 1.0 + 0.05 * nrm(k, shape, F32)

    page_table = jax.random.permutation(ks[0], n_pool)[:used].reshape(DEC_BATCH, N_PAGES).astype(jnp.int32)
    return {
        "x_prompt": nrm(ks[1], (BATCH, SEQ, D_MODEL), F32),
        "x_sample": nrm(ks[2], (DEC_BATCH, DEC_SEQ, D_MODEL), F32),
        "cache_k": nrm(ks[3], (DEPTH, n_pool, PAGE_SIZE, H_A, HEAD_DIM), F32),
        "cache_v": nrm(ks[4], (DEPTH, n_pool, PAGE_SIZE, H_A, HEAD_DIM), F32),
        "cache_kidx": nrm(ks[5], (DEPTH, n_pool, PAGE_SIZE, D_IDX), F32),
        "state_ret": nrm(ks[6], (DEPTH, DEC_BATCH, H_B, HEAD_DIM, HEAD_DIM), F32),
        "page_table": page_table,
        "w_in": nrm(ks[7], (DEPTH, D_MODEL, N_IN), F32) * D_MODEL ** -0.5,
        "w_out": nrm(ks[8], (DEPTH, D_MODEL, D_MODEL), F32) * D_MODEL ** -0.5,
        "g_ret": gain(ks[9], (DEPTH, W_B)),
        "g_mix_pre": gain(ks[10], (DEPTH, D_MODEL)),
        "g_mix_post": gain(ks[11], (DEPTH, D_MODEL)),
        "g_ffn_pre": gain(ks[12], (DEPTH, D_MODEL)),
        "g_ffn_post": gain(ks[13], (DEPTH, D_MODEL)),
        "w_up": nrm(ks[14], (DEPTH, D_MODEL, D_FF), F32) * D_MODEL ** -0.5,
        "w_down": nrm(ks[15], (DEPTH, D_FF, D_MODEL), F32) * D_FF ** -0.5,
    }


def reference(x_prompt, x_sample, cache_k, cache_v, cache_kidx, state_ret, page_table,
              w_in, w_out, g_ret, g_mix_pre, g_mix_post, g_ffn_pre, g_ffn_post, w_up, w_down):
    weights = (w_in, w_out, g_ret, g_mix_pre, g_mix_post, g_ffn_pre, g_ffn_post, w_up, w_down)
    past = page_table.shape[1] * PAGE_SIZE
    pos_p = jnp.arange(x_prompt.shape[1], dtype=F32)
    pos_s = past + jnp.arange(x_sample.shape[1], dtype=F32)
    s0_p = jnp.zeros((x_prompt.shape[0], H_B, HEAD_DIM, HEAD_DIM), x_prompt.dtype)
    yp, ys = x_prompt, x_sample
    kp, vp, kip, sp, kss, vss, kis, ss = [], [], [], [], [], [], [], []
    for l in range(DEPTH):
        yp, k_, v_, ki_, S_ = trunk_layer(yp, l, pos_p, dsa_prompt, s0_p, *weights)
        kp.append(k_); vp.append(v_); kip.append(ki_); sp.append(S_)
        mix_s = functools.partial(dsa_sample, pool_k=cache_k[l], pool_v=cache_v[l],
                                  pool_ki=cache_kidx[l], page_table=page_table)
        ys, k_, v_, ki_, S_ = trunk_layer(ys, l, pos_s, mix_s, state_ret[l], *weights)
        kss.append(k_); vss.append(v_); kis.append(ki_); ss.append(S_)
    return (yp, ys, jnp.stack(kp), jnp.stack(vp), jnp.stack(kip), jnp.stack(sp),
            jnp.stack(kss), jnp.stack(vss), jnp.stack(kis), jnp.stack(ss))
```

```python
import functools

import numpy as np
import jax
import jax.numpy as jnp
from jax import lax
from jax.experimental import pallas as pl
from jax.experimental.pallas import tpu as pltpu

F32 = jnp.float32
BF16 = jnp.bfloat16
I32 = jnp.int32

D_MODEL = 1024
HEAD_DIM = 64
H_A = 8
H_B = 8
H_IDX = 4
D_IDX = 64
W_A = H_A * HEAD_DIM
W_B = H_B * HEAD_DIM
D_FF = 4 * D_MODEL
TOPK_MAX = 256
PAGE_SIZE = 128
ROPE_THETA = 10000.0
EPS = 1e-6
Q_BLOCK = 128
RET_CHUNK = 128
COL_SIZES = (W_A, W_A, W_A, H_IDX * D_IDX, D_IDX, H_IDX, W_B, W_B, W_B, W_B)

LANES = 128
NEG = -1e30
INT_MIN = -2147483648
VMEM_LIMIT = 48 * 1024 * 1024
SAMPLE_ATTN_VMEM = 56 * 1024 * 1024

NT = (((1,), (1,)), ((), ()))
TN = (((0,), (0,)), ((), ()))

C_QA, C_KA, C_VA, C_QI, C_KIWI, C_QB, C_KB, C_VB, C_GB = (
    0, 512, 1024, 1536, 2048, 2176, 2688, 3200, 3712)
N_PACK = 4224


def _cparams(sem):
    return pltpu.CompilerParams(dimension_semantics=sem, vmem_limit_bytes=VMEM_LIMIT)


def _rms(x, g):
    return x * lax.rsqrt(jnp.mean(x * x, axis=-1, keepdims=True) + EPS) * g


def _sort_key(x):
    bits = pltpu.bitcast(x, I32)
    return bits ^ ((bits >> 31) & 0x7FFFFFFF)


def _proj_kernel(x_ref, g_ref, w_ref, cos_ref, sin_ref,
                 qa_ref, ka_ref, kabf_ref, va_ref, vabf_ref, qi_ref, kiwi_ref, kiwibf_ref,
                 ki_ref, qb_ref, kb_ref, vb_ref, gb_ref):
    x = x_ref[...]
    h = _rms(x, g_ref[...]).astype(BF16)
    cos = cos_ref[...]
    sin = sin_ref[...]
    lane = lax.broadcasted_iota(I32, cos.shape, 1)
    first_half = (lane % 64) < 32

    def rope(z, c, s):
        sw = jnp.where(first_half, pltpu.roll(z, 96, 1), pltpu.roll(z, 32, 1))
        return z * c + sw * s

    def proj(c0, n):
        return jnp.dot(h, w_ref[:, c0:c0 + n], preferred_element_type=F32)

    def roped(c0, scale):
        z = proj(c0, 512)
        parts = []
        for j in range(4):
            r = rope(z[:, j * LANES:(j + 1) * LANES], cos, sin)
            parts.append(r if scale is None else r * scale)
        return parts

    qa = roped(C_QA, HEAD_DIM ** -0.5)
    for j in range(4):
        qa_ref[:, j * LANES:(j + 1) * LANES] = qa[j].astype(BF16)
    ka = roped(C_KA, None)
    for j in range(4):
        ka_ref[:, j * LANES:(j + 1) * LANES] = ka[j]
        kabf_ref[:, j * LANES:(j + 1) * LANES] = ka[j].astype(BF16)
    va = proj(C_VA, 512)
    va_ref[...] = va
    vabf_ref[...] = va.astype(BF16)
    qi = roped(C_QI, None)
    for j in range(4):
        qi_ref[:, j * LANES:(j + 1) * LANES] = qi[j].astype(BF16)
    zk = proj(C_KIWI, LANES)
    is_key = lane < 64
    kiwi = rope(zk, jnp.where(is_key, cos, (H_IDX ** -0.5) * (D_IDX ** -0.5)),
                jnp.where(is_key, sin, 0.0))
    kiwi_ref[...] = kiwi
    kiwibf_ref[...] = kiwi.astype(BF16)
    ki_ref[...] = kiwi[:, 0:64]
    qb = roped(C_QB, None)
    for j in range(4):
        qb_ref[:, j * LANES:(j + 1) * LANES] = qb[j]
    kb = roped(C_KB, HEAD_DIM ** -0.5)
    for j in range(4):
        kb_ref[:, j * LANES:(j + 1) * LANES] = kb[j]
    vb_ref[...] = proj(C_VB, 512)
    gb_ref[...] = proj(C_GB, 512)


def _proj_call(x, g, w_pack, cos_t, sin_t, tm):
    m = x.shape[0]
    n_t = cos_t.shape[0] // tm
    row = lambda i: (i, 0)
    const = lambda i: (0, 0)
    tab = lambda i: (i % n_t, 0)
    wide = lambda dt: jax.ShapeDtypeStruct((m, 512), dt)
    out_shape = (wide(BF16), wide(F32), wide(BF16), wide(F32), wide(BF16), wide(BF16),
                 jax.ShapeDtypeStruct((m, LANES), F32), jax.ShapeDtypeStruct((m, LANES), BF16),
                 jax.ShapeDtypeStruct((m, 64), F32), wide(F32), wide(F32), wide(F32), wide(F32))
    out_specs = tuple(pl.BlockSpec((tm, s.shape[1]), row) for s in out_shape)
    return pl.pallas_call(
        _proj_kernel, grid=(m // tm,),
        in_specs=[pl.BlockSpec((tm, D_MODEL), row),
                  pl.BlockSpec((1, D_MODEL), const),
                  pl.BlockSpec((D_MODEL, N_PACK), const, pipeline_mode=pl.Buffered(1)),
                  pl.BlockSpec((tm, LANES), tab),
                  pl.BlockSpec((tm, LANES), tab)],
        out_specs=out_specs, out_shape=out_shape,
        compiler_params=_cparams(("arbitrary",)), name="proj")(x, g, w_pack, cos_t, sin_t)


def _count(key_scr, nc, ch, pred):
    def body(c, acc):
        m = jnp.where(pred(key_scr[c], c), 1.0, 0.0)
        return acc + jnp.sum(m.reshape(ch // 32, 32, LANES), axis=0)

    acc = lax.fori_loop(0, nc, body, jnp.zeros((32, LANES), F32))
    return jnp.sum(acc, axis=0, keepdims=True)


def _topk_threshold(key_scr, nc, ch, topk):
    kf = float(topk)
    krow = lax.broadcasted_iota(I32, (ch, LANES), 0)
    cnt0 = _count(key_scr, nc, ch, lambda k, c: k >= 0)
    lo0 = jnp.where(cnt0 >= kf, 0, INT_MIN)

    def bit_body(j, lo):
        cand = lo + jnp.left_shift(jnp.int32(1), 30 - j)
        cnt = _count(key_scr, nc, ch, lambda k, c: k >= cand)
        return jnp.where(cnt >= kf, cand, lo)

    thr = lax.fori_loop(0, 31, bit_body, lo0)
    need = kf - _count(key_scr, nc, ch, lambda k, c: k > thr)

    def tie_body(j, cut):
        cand = cut + jnp.left_shift(jnp.int32(1), 12 - j)
        f = _count(key_scr, nc, ch, lambda k, c: (k == thr) & ((krow + c * ch) < cand))
        return jnp.where(f <= need, cand, cut)

    cut = lax.fori_loop(0, 13, tie_body, jnp.zeros((1, LANES), I32))
    return thr, cut


DSA_CH = 512


def _dsa_prompt_kernel(qa_ref, qi_ref, kiwi_ref, ka_ref, va_ref, kibf_ref, o_ref,
                       key_scr, bias_scr, *, topk):
    ch = DSA_CH
    i = pl.program_id(1)
    t0 = i * Q_BLOCK
    nc = lax.div(t0 + Q_BLOCK + ch - 1, ch)
    qpos = t0 + lax.broadcasted_iota(I32, (ch, LANES), 1)
    krow = lax.broadcasted_iota(I32, (ch, LANES), 0)
    wi_t = kiwi_ref[...].T

    def score_body(c, carry):
        off = pl.multiple_of(c * ch, ch)
        kic = kibf_ref[pl.ds(off, ch), :]
        acc = jnp.zeros((ch, LANES), F32)
        for h in range(H_IDX):
            d = lax.dot_general(kic, qi_ref[:, h * LANES:(h + 1) * LANES], NT,
                                preferred_element_type=F32)
            acc = acc + wi_t[64 + h:65 + h, :] * jnp.maximum(d, 0.0)
        acc = jnp.where(acc == 0.0, 0.0, acc)
        sc = jnp.where(krow + c * ch <= qpos, acc, -jnp.inf)
        key_scr[c] = _sort_key(sc)
        return carry

    lax.fori_loop(0, nc, score_body, 0)
    thr, cut = _topk_threshold(key_scr, nc, ch, topk)

    def bias_body(c, carry):
        k = key_scr[c]
        kpos = krow + c * ch
        sel = ((k > thr) | ((k == thr) & (kpos < cut))) & (kpos <= qpos)
        bias_scr[c] = jnp.where(sel, 0.0, NEG)
        return carry

    lax.fori_loop(0, nc, bias_body, 0)

    lane_lo = lax.broadcasted_iota(I32, (Q_BLOCK, LANES), 1) < HEAD_DIM
    row_lo = lax.broadcasted_iota(I32, (LANES, Q_BLOCK), 0) < HEAD_DIM
    for p in range(H_A // 2):
        cols = slice(p * LANES, (p + 1) * LANES)
        qp = qa_ref[:, cols]
        q_even = jnp.where(lane_lo, qp, jnp.zeros_like(qp))
        q_odd = jnp.where(lane_lo, jnp.zeros_like(qp), qp)

        def att_body(c, carry):
            off = pl.multiple_of(c * ch, ch)
            kc = ka_ref[pl.ds(off, ch), cols]
            vc = va_ref[pl.ds(off, ch), cols]
            b = bias_scr[c]
            out = []
            for q, (m, l, a) in ((q_even, carry[0:3]), (q_odd, carry[3:6])):
                s = lax.dot_general(kc, q, NT, preferred_element_type=F32) + b
                m_new = jnp.maximum(m, jnp.max(s, axis=0, keepdims=True))
                alpha = jnp.exp(m - m_new)
                pt = jnp.exp(s - m_new)
                l = alpha * l + jnp.sum(pt, axis=0, keepdims=True)
                a = alpha * a + lax.dot_general(vc, pt.astype(BF16), TN,
                                                preferred_element_type=F32)
                out += [m_new, l, a]
            return tuple(out)

        init = (jnp.full((1, LANES), NEG, F32), jnp.zeros((1, LANES), F32),
                jnp.zeros((LANES, Q_BLOCK), F32)) * 2
        me, le, ae, mo, lo, ao = lax.fori_loop(0, nc, att_body, init)
        o_t = jnp.where(row_lo, ae / le, ao / lo)
        o_ref[:, cols] = o_t.T.astype(BF16)


def _dsa_prompt_call(qa, qi, kiwi, ka_bf, va_bf, kiwi_bf, batch, seq):
    nq = seq // Q_BLOCK
    topk = min(TOPK_MAX, seq // 4)
    nch = -(-seq // DSA_CH)
    blk = lambda b, i: (b * nq + i, 0)
    full = lambda b, i: (b, 0)
    return pl.pallas_call(
        functools.partial(_dsa_prompt_kernel, topk=topk), grid=(batch, nq),
        in_specs=[pl.BlockSpec((Q_BLOCK, W_A), blk),
                  pl.BlockSpec((Q_BLOCK, 512), blk),
                  pl.BlockSpec((Q_BLOCK, LANES), blk),
                  pl.BlockSpec((seq, W_A), full),
                  pl.BlockSpec((seq, W_A), full),
                  pl.BlockSpec((seq, LANES), full)],
        out_specs=pl.BlockSpec((Q_BLOCK, W_A), blk),
        out_shape=jax.ShapeDtypeStruct((batch * seq, W_A), BF16),
        scratch_shapes=[pltpu.VMEM((nch, DSA_CH, LANES), I32),
                        pltpu.VMEM((nch, DSA_CH, LANES), F32)],
        compiler_params=_cparams(("arbitrary", "arbitrary")),
        name="dsa_prompt")(qa, qi, kiwi, ka_bf, va_bf, kiwi_bf)


def _group_norm_gate(o, gret, gate):
    mu = jnp.mean(o, axis=-1, keepdims=True)
    var = jnp.mean(jnp.square(o - mu), axis=-1, keepdims=True)
    on = (o - mu) * lax.rsqrt(var + EPS)
    return on * gret * (gate * (1.0 / (1.0 + jnp.exp(-gate))))


def _ret_prompt_kernel(q_ref, k_ref, v_ref, g_ref, dmask_ref, qdec_ref, kdec_ref, cdec_ref,
                       gret_ref, o_ref, sout_ref, s_scr, o_scr):
    c = pl.program_id(1)

    @pl.when(c == 0)
    def _():
        s_scr[...] = jnp.zeros_like(s_scr)

    q = q_ref[...]
    k = k_ref[...]
    v = v_ref[...]
    kd_t = (k * kdec_ref[...]).T.astype(BF16)
    for h in range(H_B):
        sl = slice(h * HEAD_DIM, (h + 1) * HEAD_DIM)
        qh = q[:, sl].astype(BF16)
        kh = k[:, sl].astype(BF16)
        vh = v[:, sl].astype(BF16)
        att = lax.dot_general(qh, kh, NT, preferred_element_type=F32) * dmask_ref[h]
        s_h = s_scr[h]
        o = (jnp.dot(att.astype(BF16), vh, preferred_element_type=F32)
             + jnp.dot(qh, s_h.astype(BF16), preferred_element_type=F32) * qdec_ref[:, sl])
        s_scr[h] = s_h * cdec_ref[h] + jnp.dot(kd_t[sl, :], vh, preferred_element_type=F32)
        o_scr[:, sl] = _group_norm_gate(o, gret_ref[:, sl], g_ref[:, sl])
    o_ref[...] = o_scr[...].astype(BF16)

    @pl.when(c == pl.num_programs(1) - 1)
    def _():
        sout_ref[...] = s_scr[...]


def _ret_tables(chunk):
    lg = jnp.log1p(-jnp.exp2(-5.0 - jnp.arange(H_B, dtype=F32)))
    i = jnp.arange(chunk, dtype=F32)
    diff = i[:, None] - i[None, :]
    dmask = jnp.where(diff[None] >= 0, jnp.exp(jnp.maximum(diff, 0.0)[None] * lg[:, None, None]), 0.0)
    q_dec = jnp.exp((i[:, None] + 1.0) * lg[None, :])
    k_dec = jnp.exp((chunk - 1.0 - i)[:, None] * lg[None, :])
    c_dec = jnp.exp(chunk * lg)
    return dmask, q_dec, k_dec, c_dec


def _ret_prompt_call(qb, kb, vb, gb, g_ret, batch, seq):
    chunk = RET_CHUNK
    n = seq // chunk
    dmask, q_dec, k_dec, c_dec = _ret_tables(chunk)
    qdec = jnp.repeat(q_dec, HEAD_DIM, axis=1)
    kdec = jnp.repeat(k_dec, HEAD_DIM, axis=1)
    cdec = jnp.broadcast_to(c_dec[:, None, None], (H_B, HEAD_DIM, HEAD_DIM))
    blk = lambda b, c: (b * n + c, 0)
    c2 = lambda b, c: (0, 0)
    c3 = lambda b, c: (0, 0, 0)
    return pl.pallas_call(
        _ret_prompt_kernel, grid=(batch, n),
        in_specs=[pl.BlockSpec((chunk, W_B), blk)] * 4 + [
            pl.BlockSpec((H_B, chunk, chunk), c3),
            pl.BlockSpec((chunk, W_B), c2),
            pl.BlockSpec((chunk, W_B), c2),
            pl.BlockSpec((H_B, HEAD_DIM, HEAD_DIM), c3),
            pl.BlockSpec((1, W_B), c2)],
        out_specs=(pl.BlockSpec((chunk, W_B), blk),
                   pl.BlockSpec((None, H_B, HEAD_DIM, HEAD_DIM), lambda b, c: (b, 0, 0, 0))),
        out_shape=(jax.ShapeDtypeStruct((batch * seq, W_B), BF16),
                   jax.ShapeDtypeStruct((batch, H_B, HEAD_DIM, HEAD_DIM), F32)),
        scratch_shapes=[pltpu.VMEM((H_B, HEAD_DIM, HEAD_DIM), F32),
                        pltpu.VMEM((chunk, W_B), F32)],
        compiler_params=_cparams(("arbitrary", "arbitrary")),
        name="ret_prompt")(qb, kb, vb, gb, dmask, qdec, kdec, cdec, g_ret.reshape(1, W_B))


FF_CHUNK = 1024


def _outffn_kernel(a_ref, b_ref, x_ref, wo_ref, gpost_ref, gfpre_ref, gfpost_ref,
                   wup_ref, wdn_ref, y_ref):
    m = (jnp.dot(a_ref[...], wo_ref[0:W_A, :], preferred_element_type=F32)
         + jnp.dot(b_ref[...], wo_ref[W_A:D_MODEL, :], preferred_element_type=F32))
    x1 = x_ref[...] + _rms(m, gpost_ref[...])
    hn = _rms(x1, gfpre_ref[...]).astype(BF16)
    f = jnp.zeros(x1.shape, F32)
    for c in range(D_FF // FF_CHUNK):
        cs = slice(c * FF_CHUNK, (c + 1) * FF_CHUNK)
        u = jnp.dot(hn, wup_ref[:, cs], preferred_element_type=F32)
        u = jnp.square(jnp.maximum(u, 0.0)).astype(BF16)
        f = f + jnp.dot(u, wdn_ref[cs, :], preferred_element_type=F32)
    y_ref[...] = x1 + _rms(f, gfpost_ref[...])


def _outffn_call(a, b, x, wo, gpost, gfpre, gfpost, wup, wdn, tm):
    m = x.shape[0]
    row = lambda i: (i, 0)
    const = lambda i: (0, 0)
    once = dict(pipeline_mode=pl.Buffered(1))
    return pl.pallas_call(
        _outffn_kernel, grid=(m // tm,),
        in_specs=[pl.BlockSpec((tm, W_A), row), pl.BlockSpec((tm, W_B), row),
                  pl.BlockSpec((tm, D_MODEL), row),
                  pl.BlockSpec((D_MODEL, D_MODEL), const, **once),
                  pl.BlockSpec((1, D_MODEL), const), pl.BlockSpec((1, D_MODEL), const),
                  pl.BlockSpec((1, D_MODEL), const),
                  pl.BlockSpec((D_MODEL, D_FF), const, **once),
                  pl.BlockSpec((D_FF, D_MODEL), const, **once)],
        out_specs=pl.BlockSpec((tm, D_MODEL), row),
        out_shape=jax.ShapeDtypeStruct((m, D_MODEL), F32),
        compiler_params=_cparams(("arbitrary",)),
        name="outffn")(a, b, x, wo, gpost.reshape(1, -1), gfpre.reshape(1, -1),
                       gfpost.reshape(1, -1), wup, wdn)


def _dsa_sample_scores_kernel(pt_ref, *refs, n_pages):
    pages = refs[:n_pages]
    qi_ref, wi_ref, kinew_ref, o_ref = refs[n_pages:]
    qi = qi_ref[...]
    wi = wi_ref[...]

    def score_row(kp):
        d = lax.dot_general(qi, kp, NT, preferred_element_type=F32)
        r = jnp.sum(wi * jnp.maximum(d, 0.0), axis=0, keepdims=True)
        return jnp.where(r == 0.0, 0.0, r)

    for p in range(n_pages):
        o_ref[:, p * LANES:(p + 1) * LANES] = score_row(pages[p][...].astype(BF16))
    knew = jnp.broadcast_to(kinew_ref[...], (PAGE_SIZE, D_IDX)).astype(BF16)
    lane = lax.broadcasted_iota(I32, (1, LANES), 1)
    o_ref[:, n_pages * LANES:(n_pages + 1) * LANES] = jnp.where(lane == 0, score_row(knew), -jnp.inf)


def _page_spec(layer, p, width):
    return pl.BlockSpec((None, None, PAGE_SIZE, width), lambda b, pt: (layer, pt[b, p], 0, 0))


def _dsa_sample_scores_call(page_table, pool_ki, layer, qi16, wi16, ki_new):
    nb, n_pages = page_table.shape
    ncol = (n_pages + 1) * LANES
    per = lambda b, pt: (b, 0, 0)
    return pl.pallas_call(
        functools.partial(_dsa_sample_scores_kernel, n_pages=n_pages),
        grid_spec=pltpu.PrefetchScalarGridSpec(
            num_scalar_prefetch=1, grid=(nb,),
            in_specs=[_page_spec(layer, p, D_IDX) for p in range(n_pages)] + [
                pl.BlockSpec((None, 16, D_IDX), per),
                pl.BlockSpec((None, 16, 1), per),
                pl.BlockSpec((None, 1, D_IDX), per)],
            out_specs=pl.BlockSpec((None, 1, ncol), per)),
        out_shape=jax.ShapeDtypeStruct((nb, 1, ncol), F32),
        compiler_params=_cparams(("arbitrary",)),
        name="dsa_sample_scores")(page_table, *([pool_ki] * n_pages), qi16, wi16, ki_new)


def _dsa_sample_select_kernel(sc_ref, sel_ref, key_scr, *, topk, n_valid):
    ncol = sc_ref.shape[1]
    for p in range(ncol // LANES):
        rows = slice(p * LANES, (p + 1) * LANES)
        key_scr[0, rows, :] = _sort_key(sc_ref[:, rows].T)
    thr, cut = _topk_threshold(key_scr, 1, ncol, topk)
    rep_shape = (LANES, LANES * H_A)
    rep = jnp.where(lax.shift_right_logical(lax.broadcasted_iota(I32, rep_shape, 1), 3)
                    == lax.broadcasted_iota(I32, rep_shape, 0), 1.0, 0.0).astype(BF16)
    krow = lax.broadcasted_iota(I32, (LANES, LANES), 0)
    for p in range(ncol // LANES):
        k = key_scr[0, p * LANES:(p + 1) * LANES, :]
        kpos = krow + p * LANES
        sel = ((k > thr) | ((k == thr) & (kpos < cut))) & (kpos < n_valid)
        sel_rows = jnp.where(sel, 1.0, 0.0).T.astype(BF16)
        sel_ref[:, p * LANES * H_A:(p + 1) * LANES * H_A] = jnp.dot(
            sel_rows, rep, preferred_element_type=F32)


def _dsa_sample_select_call(sc, topk, n_valid):
    nb, ncol = sc.shape
    return pl.pallas_call(
        functools.partial(_dsa_sample_select_kernel, topk=topk, n_valid=n_valid),
        out_shape=jax.ShapeDtypeStruct((nb, ncol * H_A), F32),
        scratch_shapes=[pltpu.VMEM((1, ncol, nb), I32)],
        compiler_params=pltpu.CompilerParams(vmem_limit_bytes=VMEM_LIMIT),
        name="dsa_sample_select")(sc)


def _dsa_sample_attn_kernel(pt_ref, *refs, n_pages):
    kpages = refs[:n_pages]
    vpages = refs[n_pages:2 * n_pages]
    q_ref, knew_ref, vnew_ref, sel_ref, o_ref = refs[2 * n_pages:]
    flat = PAGE_SIZE * H_A
    q16 = jnp.concatenate([q_ref[...], jnp.zeros((16 - H_A, HEAD_DIM), F32)], axis=0).astype(BF16)
    row = lax.broadcasted_iota(I32, (16, flat), 0)
    col = lax.broadcasted_iota(I32, (16, flat), 1)
    own_head = (col & (H_A - 1)) == row
    first_pos = lax.broadcasted_iota(I32, (PAGE_SIZE, H_A, HEAD_DIM), 0) == 0

    def tile(pages, new_ref, p):
        if p < n_pages:
            t = pages[p][...]
        else:
            t = jnp.where(first_pos, jnp.broadcast_to(new_ref[...][None], first_pos.shape), 0.0)
        return t.reshape(flat, HEAD_DIM).astype(BF16)

    s_all = []
    for p in range(n_pages + 1):
        s = lax.dot_general(q16, tile(kpages, knew_ref, p), NT, preferred_element_type=F32)
        valid = own_head & (sel_ref[:, p * flat:(p + 1) * flat] > 0.5)
        s_all.append(jnp.where(valid, s, NEG))
    m = s_all[0]
    for s in s_all[1:]:
        m = jnp.maximum(m, s)
    m = jnp.max(m, axis=1, keepdims=True)
    p_all = [jnp.exp(s - m) for s in s_all]
    l = p_all[0]
    for pp in p_all[1:]:
        l = l + pp
    inv = 1.0 / jnp.sum(l, axis=1, keepdims=True)
    acc = jnp.zeros((16, HEAD_DIM), F32)
    for p in range(n_pages + 1):
        acc = acc + jnp.dot((p_all[p] * inv).astype(BF16), tile(vpages, vnew_ref, p),
                            preferred_element_type=F32)
    o_ref[...] = acc[0:H_A]


def _dsa_sample_attn_call(page_table, pool_k, pool_v, layer, q, k_new, v_new, sel_x):
    nb, n_pages = page_table.shape
    per = lambda b, pt: (b, 0, 0)
    page = lambda p: pl.BlockSpec((None, None, PAGE_SIZE, H_A, HEAD_DIM),
                                  lambda b, pt: (layer, pt[b, p], 0, 0, 0))
    head_blk = pl.BlockSpec((None, H_A, HEAD_DIM), per)
    return pl.pallas_call(
        functools.partial(_dsa_sample_attn_kernel, n_pages=n_pages),
        grid_spec=pltpu.PrefetchScalarGridSpec(
            num_scalar_prefetch=1, grid=(nb,),
            in_specs=[page(p) for p in range(n_pages)] * 2 + [
                head_blk, head_blk, head_blk,
                pl.BlockSpec((None, 1, sel_x.shape[2]), per)],
            out_specs=head_blk),
        out_shape=jax.ShapeDtypeStruct((nb, H_A, HEAD_DIM), F32),
        compiler_params=pltpu.CompilerParams(dimension_semantics=("arbitrary",),
                                             vmem_limit_bytes=SAMPLE_ATTN_VMEM),
        name="dsa_sample_attn")(page_table, *([pool_k] * n_pages), *([pool_v] * n_pages),
                                q, k_new, v_new, sel_x)


def _ret_sample_kernel(q_ref, k_ref, v_ref, g_ref, s_ref, gret_ref, gam_ref, o_ref, sout_ref,
                       qt_scr, kt_scr):
    b = pl.program_id(0)

    @pl.when(b == 0)
    def _():
        qt_scr[...] = q_ref[...].T
        kt_scr[...] = k_ref[...].T

    nb = q_ref.shape[0]
    onehot = jnp.where(lax.broadcasted_iota(I32, (nb, HEAD_DIM), 0) == b, 1.0, 0.0)
    hi = lax.Precision.HIGHEST
    qcol = jnp.dot(qt_scr[...], onehot, preferred_element_type=F32, precision=hi)
    kcol = jnp.dot(kt_scr[...], onehot, preferred_element_type=F32, precision=hi)
    v = v_ref[...]
    vexp = jnp.broadcast_to(v[:, None, :], (H_B, HEAD_DIM, HEAD_DIM)).reshape(W_B, HEAD_DIM)
    s_new = s_ref[...] * gam_ref[...] + kcol * vexp
    sout_ref[...] = s_new
    o = jnp.sum((qcol * s_new).reshape(H_B, HEAD_DIM, HEAD_DIM), axis=1)
    o_ref[...] = _group_norm_gate(o, gret_ref[...], g_ref[...])


def _ret_sample_call(qb, kb, vb, gb, state, g_ret):
    nb = qb.shape[0]
    lg = jnp.log1p(-jnp.exp2(-5.0 - jnp.arange(H_B, dtype=F32)))
    gam = jnp.exp(1.0 * lg)
    gam_rows = jnp.broadcast_to(jnp.repeat(gam, HEAD_DIM)[:, None], (W_B, HEAD_DIM))
    full = lambda b: (0, 0)
    per = lambda b: (b, 0, 0)
    o, s_new = pl.pallas_call(
        _ret_sample_kernel, grid=(nb,),
        in_specs=[pl.BlockSpec((nb, W_B), full), pl.BlockSpec((nb, W_B), full),
                  pl.BlockSpec((None, H_B, HEAD_DIM), per), pl.BlockSpec((None, H_B, HEAD_DIM), per),
                  pl.BlockSpec((None, W_B, HEAD_DIM), per),
                  pl.BlockSpec((H_B, HEAD_DIM), full), pl.BlockSpec((W_B, HEAD_DIM), full)],
        out_specs=(pl.BlockSpec((None, H_B, HEAD_DIM), per),
                   pl.BlockSpec((None, W_B, HEAD_DIM), per)),
        out_shape=(jax.ShapeDtypeStruct((nb, H_B, HEAD_DIM), F32),
                   jax.ShapeDtypeStruct((nb, W_B, HEAD_DIM), F32)),
        scratch_shapes=[pltpu.VMEM((W_B, nb), F32), pltpu.VMEM((W_B, nb), F32)],
        compiler_params=_cparams(("arbitrary",)),
        name="ret_sample")(qb, kb, vb.reshape(nb, H_B, HEAD_DIM), gb.reshape(nb, H_B, HEAD_DIM),
                           state.reshape(nb, W_B, HEAD_DIM), g_ret.reshape(H_B, HEAD_DIM), gam_rows)
    return o.reshape(nb, W_B).astype(BF16), s_new.reshape(nb, H_B, HEAD_DIM, HEAD_DIM)


def _pack_w_in(w):
    d = w.shape[0]
    splits = np.cumsum(COL_SIZES)[:-1].tolist()
    qa, ka, va, qi, ki, wi, qb, kb, vb, gb = jnp.split(w, splits, axis=1)
    qi_pad = jnp.pad(qi.reshape(d, H_IDX, D_IDX), ((0, 0), (0, 0), (0, LANES - D_IDX))).reshape(d, -1)
    kiwi = jnp.concatenate([ki, wi, jnp.zeros((d, LANES - D_IDX - H_IDX), w.dtype)], axis=1)
    return jnp.concatenate([qa, ka, va, qi_pad, kiwi, qb, kb, vb, gb], axis=1).astype(BF16)


def _rope_tables(pos):
    half = HEAD_DIM // 2
    inv = ROPE_THETA ** (-jnp.arange(half, dtype=F32) / half)
    ang = pos[:, None] * inv[None, :]
    c = jnp.cos(ang)
    s = jnp.sin(ang)
    return jnp.concatenate([c, c, c, c], axis=1), jnp.concatenate([-s, s, -s, s], axis=1)


def _layer_weights(l, w_in, w_out, w_up, w_down):
    return (_pack_w_in(w_in[l]), w_out[l].astype(BF16), w_up[l].astype(BF16), w_down[l].astype(BF16))


def kernel(x_prompt, x_sample, cache_k, cache_v, cache_kidx, state_ret, page_table, w_in, w_out,
           g_ret, g_mix_pre, g_mix_post, g_ffn_pre, g_ffn_post, w_up, w_down):
    batch, seq, _ = x_prompt.shape
    nb, dec_seq, _ = x_sample.shape
    depth = w_in.shape[0]
    n_pages = page_table.shape[1]
    past = n_pages * PAGE_SIZE
    assert dec_seq == 1 and seq % DSA_CH == 0 and nb == LANES
    tm_p = 256

    cos_p, sin_p = _rope_tables(jnp.arange(seq, dtype=F32))
    cos_s, sin_s = _rope_tables(past + jnp.arange(dec_seq, dtype=F32))
    cos_s = jnp.broadcast_to(cos_s, (nb, LANES))
    sin_s = jnp.broadcast_to(sin_s, (nb, LANES))
    topk_s = min(TOPK_MAX, (past + dec_seq) // 4)

    yp = x_prompt.reshape(batch * seq, D_MODEL)
    ys = x_sample.reshape(nb, D_MODEL)
    kp, vp, kip, sp, kss, vss, kis, ss = [], [], [], [], [], [], [], []
    for l in range(depth):
        w_pack, wo, wup, wdn = _layer_weights(l, w_in, w_out, w_up, w_down)
        gpre = g_mix_pre[l].reshape(1, D_MODEL)

        (qa, ka, ka_bf, va, va_bf, qi, kiwi, kiwi_bf, ki, qb, kb, vb, gb) = _proj_call(
            yp, gpre, w_pack, cos_p, sin_p, tm_p)
        a = _dsa_prompt_call(qa, qi, kiwi, ka_bf, va_bf, kiwi_bf, batch, seq)
        bmix, s_new = _ret_prompt_call(qb, kb, vb, gb, g_ret[l], batch, seq)
        yp = _outffn_call(a, bmix, yp, wo, g_mix_post[l], g_ffn_pre[l], g_ffn_post[l], wup, wdn, tm_p)
        kp.append(ka); vp.append(va); kip.append(ki); sp.append(s_new)

        (qa, ka, ka_bf, va, va_bf, qi, kiwi, kiwi_bf, ki, qb, kb, vb, gb) = _proj_call(
            ys, gpre, w_pack, cos_s, sin_s, nb)
        qi16 = jnp.pad(qi.reshape(nb, H_IDX, LANES)[:, :, :D_IDX], ((0, 0), (0, 16 - H_IDX), (0, 0)))
        wi16 = jnp.pad(kiwi[:, D_IDX:D_IDX + H_IDX], ((0, 0), (0, 16 - H_IDX)))[:, :, None]
        sc = _dsa_sample_scores_call(page_table, cache_kidx, l, qi16, wi16, ki.reshape(nb, 1, D_IDX))
        sel_x = _dsa_sample_select_call(sc.reshape(nb, -1), topk_s, past + dec_seq)
        a = _dsa_sample_attn_call(page_table, cache_k, cache_v, l,
                                  qa.astype(F32).reshape(nb, H_A, HEAD_DIM),
                                  ka.reshape(nb, H_A, HEAD_DIM), va.reshape(nb, H_A, HEAD_DIM),
                                  sel_x.reshape(nb, 1, -1))
        bmix, s_new = _ret_sample_call(qb, kb, vb, gb, state_ret[l], g_ret[l])
        ys = _outffn_call(a.reshape(nb, W_A).astype(BF16), bmix, ys, wo, g_mix_post[l], g_ffn_pre[l],
                          g_ffn_post[l], wup, wdn, nb)
        kss.append(ka); vss.append(va); kis.append(ki); ss.append(s_new)

    def heads(xs, lead):
        return jnp.stack(xs).reshape((depth,) + lead + (H_A, HEAD_DIM))

    return (yp.reshape(batch, seq, D_MODEL), ys.reshape(nb, dec_seq, D_MODEL),
            heads(kp, (batch, seq)), heads(vp, (batch, seq)),
            jnp.stack(kip).reshape(depth, batch, seq, D_IDX), jnp.stack(sp),
            heads(kss, (nb, dec_seq)), heads(vss, (nb, dec_seq)),
            jnp.stack(kis).reshape(depth, nb, dec_seq, D_IDX), jnp.stack(ss))
```

```python
import functools

import numpy as np
import jax
import jax.numpy as jnp
from jax import lax
from jax.experimental import pallas as pl
from jax.experimental.pallas import tpu as pltpu

F32 = jnp.float32
BF16 = jnp.bfloat16
I32 = jnp.int32

D_MODEL = 1024
HEAD_DIM = 64
H_A = 8
H_B = 8
H_IDX = 4
D_IDX = 64
W_A = H_A * HEAD_DIM
W_B = H_B * HEAD_DIM
D_FF = 4 * D_MODEL
TOPK_MAX = 256
PAGE_SIZE = 128
ROPE_THETA = 10000.0
EPS = 1e-6
Q_BLOCK = 128
RET_CHUNK = 128
COL_SIZES = (W_A, W_A, W_A, H_IDX * D_IDX, D_IDX, H_IDX, W_B, W_B, W_B, W_B)

LANES = 128
NEG = -1e30
INT_MIN = -2147483648
VMEM_LIMIT = 48 * 1024 * 1024

NT = (((1,), (1,)), ((), ()))

C_QA, C_KA, C_VA, C_QI, C_KIWI, C_QB, C_KB, C_VB, C_GB = (
    0, 512, 1024, 1536, 2048, 2176, 2688, 3200, 3712)
N_PACK = 4224


def _cparams(sem):
    return pltpu.CompilerParams(dimension_semantics=sem, vmem_limit_bytes=VMEM_LIMIT)


def _rms(x, g):
    return x * lax.rsqrt(jnp.mean(x * x, axis=-1, keepdims=True) + EPS) * g


def _sort_key(x):
    bits = pltpu.bitcast(x, I32)
    return bits ^ ((bits >> 31) & 0x7FFFFFFF)


def _proj_kernel(x_ref, g_ref, w_ref, cos_ref, sin_ref,
                 qa_ref, ka_ref, kabf_ref, va_ref, vat_ref, qi_ref, kiwi_ref, kiwibf_ref,
                 ki_ref, qb_ref, kb_ref, vb_ref, gb_ref):
    x = x_ref[...]
    h = _rms(x, g_ref[...]).astype(BF16)
    cos = cos_ref[...]
    sin = sin_ref[...]
    lane = lax.broadcasted_iota(I32, cos.shape, 1)
    first_half = (lane % 64) < 32

    def rope(z, c, s):
        sw = jnp.where(first_half, pltpu.roll(z, 96, 1), pltpu.roll(z, 32, 1))
        return z * c + sw * s

    def proj(c0, n):
        return lax.dot_general(h, w_ref[c0:c0 + n, :], NT, preferred_element_type=F32)

    def roped(c0, scale):
        z = proj(c0, 512)
        parts = []
        for j in range(4):
            r = rope(z[:, j * LANES:(j + 1) * LANES], cos, sin)
            parts.append(r if scale is None else r * scale)
        return parts

    qa = roped(C_QA, HEAD_DIM ** -0.5)
    for j in range(4):
        qa_ref[:, j * LANES:(j + 1) * LANES] = qa[j].astype(BF16)
    ka = roped(C_KA, None)
    for j in range(4):
        ka_ref[:, j * LANES:(j + 1) * LANES] = ka[j]
        kabf_ref[:, j * LANES:(j + 1) * LANES] = ka[j].astype(BF16)
    va = proj(C_VA, 512)
    va_ref[...] = va
    vat_ref[...] = va.T.astype(BF16)
    qi = roped(C_QI, None)
    for j in range(4):
        qi_ref[:, j * LANES:(j + 1) * LANES] = qi[j].astype(BF16)
    zk = proj(C_KIWI, LANES)
    is_key = lane < 64
    kiwi = rope(zk, jnp.where(is_key, cos, (H_IDX ** -0.5) * (D_IDX ** -0.5)),
                jnp.where(is_key, sin, 0.0))
    kiwi_ref[...] = kiwi
    kiwibf_ref[...] = kiwi.astype(BF16)
    ki_ref[...] = kiwi[:, 0:64]
    qb = roped(C_QB, None)
    for j in range(4):
        qb_ref[:, j * LANES:(j + 1) * LANES] = qb[j]
    kb = roped(C_KB, HEAD_DIM ** -0.5)
    for j in range(4):
        kb_ref[:, j * LANES:(j + 1) * LANES] = kb[j]
    vb_ref[...] = proj(C_VB, 512)
    gb_ref[...] = proj(C_GB, 512)


def _proj_call(x, g, w_pack_t, cos_t, sin_t, tm):
    m = x.shape[0]
    n_t = cos_t.shape[0] // tm
    row = lambda i: (i, 0)
    const = lambda i: (0, 0)
    tab = lambda i: (i % n_t, 0)
    wide = lambda dt: jax.ShapeDtypeStruct((m, 512), dt)
    out_shape = (wide(BF16), wide(F32), wide(BF16), wide(F32),
                 jax.ShapeDtypeStruct((m // tm, W_A, tm), BF16), wide(BF16),
                 jax.ShapeDtypeStruct((m, LANES), F32), jax.ShapeDtypeStruct((m, LANES), BF16),
                 jax.ShapeDtypeStruct((m, 64), F32), wide(F32), wide(F32), wide(F32), wide(F32))
    out_specs = tuple(
        pl.BlockSpec((None, W_A, tm), lambda i: (i, 0, 0)) if len(s.shape) == 3
        else pl.BlockSpec((tm, s.shape[1]), row) for s in out_shape)
    return pl.pallas_call(
        _proj_kernel, grid=(m // tm,),
        in_specs=[pl.BlockSpec((tm, D_MODEL), row),
                  pl.BlockSpec((1, D_MODEL), const),
                  pl.BlockSpec((N_PACK, D_MODEL), const, pipeline_mode=pl.Buffered(1)),
                  pl.BlockSpec((tm, LANES), tab),
                  pl.BlockSpec((tm, LANES), tab)],
        out_specs=out_specs, out_shape=out_shape,
        compiler_params=_cparams(("arbitrary",)), name="proj")(x, g, w_pack_t, cos_t, sin_t)


def _count(key_scr, nc, ch, pred):
    def body(c, acc):
        m = jnp.where(pred(key_scr[c], c), 1.0, 0.0)
        return acc + jnp.sum(m.reshape(ch // 32, 32, LANES), axis=0)

    acc = lax.fori_loop(0, nc, body, jnp.zeros((32, LANES), F32))
    return jnp.sum(acc, axis=0, keepdims=True)


def _topk_threshold(key_scr, nc, ch, topk):
    kf = float(topk)
    krow = lax.broadcasted_iota(I32, (ch, LANES), 0)
    cnt0 = _count(key_scr, nc, ch, lambda k, c: k >= 0)
    lo0 = jnp.where(cnt0 >= kf, 0, INT_MIN)

    def bit_body(j, lo):
        cand = lo + jnp.left_shift(jnp.int32(1), 30 - j)
        cnt = _count(key_scr, nc, ch, lambda k, c: k >= cand)
        return jnp.where(cnt >= kf, cand, lo)

    thr = lax.fori_loop(0, 31, bit_body, lo0)
    need = kf - _count(key_scr, nc, ch, lambda k, c: k > thr)

    def tie_body(j, cut):
        cand = cut + jnp.left_shift(jnp.int32(1), 12 - j)
        f = _count(key_scr, nc, ch, lambda k, c: (k == thr) & ((krow + c * ch) < cand))
        return jnp.where(f <= need, cand, cut)

    cut = lax.fori_loop(0, 13, tie_body, jnp.zeros((1, LANES), I32))
    return thr, cut


DSA_CH = 512


def _dsa_prompt_kernel(qa_ref, qi_ref, kiwi_ref, ka_ref, vat_ref, kibf_ref, o_ref,
                       key_scr, bias_scr, acc_scr, *, topk):
    ch = DSA_CH
    n_pair = H_A // 2
    i = pl.program_id(1)
    t0 = i * Q_BLOCK
    nc = lax.div(t0 + Q_BLOCK + ch - 1, ch)
    qpos = t0 + lax.broadcasted_iota(I32, (ch, LANES), 1)
    krow = lax.broadcasted_iota(I32, (ch, LANES), 0)
    wi_t = kiwi_ref[...].T

    def score_body(c, carry):
        off = pl.multiple_of(c * ch, ch)
        kic = kibf_ref[pl.ds(off, ch), :]
        acc = jnp.zeros((ch, LANES), F32)
        for h in range(H_IDX):
            d = lax.dot_general(kic, qi_ref[:, h * LANES:(h + 1) * LANES], NT,
                                preferred_element_type=F32)
            acc = acc + wi_t[64 + h:65 + h, :] * jnp.maximum(d, 0.0)
        acc = jnp.where(acc == 0.0, 0.0, acc)
        sc = jnp.where(krow + c * ch <= qpos, acc, -jnp.inf)
        key_scr[c] = _sort_key(sc)
        return carry

    lax.fori_loop(0, nc, score_body, 0)
    thr, cut = _topk_threshold(key_scr, nc, ch, topk)

    def bias_body(c, carry):
        k = key_scr[c]
        kpos = krow + c * ch
        sel = ((k > thr) | ((k == thr) & (kpos < cut))) & (kpos <= qpos)
        bias_scr[c] = jnp.where(sel, 0.0, NEG)
        return carry

    lax.fori_loop(0, nc, bias_body, 0)

    lane_lo = lax.broadcasted_iota(I32, (Q_BLOCK, LANES), 1) < HEAD_DIM
    row_lo = lax.broadcasted_iota(I32, (LANES, Q_BLOCK), 0) < HEAD_DIM
    acc_scr[...] = jnp.zeros_like(acc_scr)

    def att_body(c, carry):
        off = pl.multiple_of(c * ch, ch)
        b = bias_scr[c]
        new = []
        for p in range(n_pair):
            cols = slice(p * LANES, (p + 1) * LANES)
            qp = qa_ref[:, cols]
            zero = jnp.zeros_like(qp)
            kc = ka_ref[pl.ds(off, ch), cols]
            vt = vat_ref[c, cols, :]
            alphas, pvs = [], []
            for e, q in enumerate((jnp.where(lane_lo, qp, zero), jnp.where(lane_lo, zero, qp))):
                m, l = carry[4 * p + 2 * e], carry[4 * p + 2 * e + 1]
                s = lax.dot_general(kc, q, NT, preferred_element_type=F32) + b
                m_new = jnp.maximum(m, jnp.max(s, axis=0, keepdims=True))
                alpha = jnp.exp(m - m_new)
                pt = jnp.exp(s - m_new)
                new += [m_new, alpha * l + jnp.sum(pt, axis=0, keepdims=True)]
                alphas.append(alpha)
                pvs.append(jnp.dot(vt, pt.astype(BF16), preferred_element_type=F32))
            acc_scr[p] = (acc_scr[p] * jnp.where(row_lo, alphas[0], alphas[1])
                          + jnp.where(row_lo, pvs[0], pvs[1]))
        return tuple(new)

    init = (jnp.full((1, LANES), NEG, F32), jnp.zeros((1, LANES), F32)) * H_A
    stats = lax.fori_loop(0, nc, att_body, init)
    for p in range(n_pair):
        l_pair = jnp.where(row_lo, stats[4 * p + 1], stats[4 * p + 3])
        o_ref[:, p * LANES:(p + 1) * LANES] = (acc_scr[p] / l_pair).T.astype(BF16)


def _dsa_prompt_call(qa, qi, kiwi, ka_bf, va_t, kiwi_bf, batch, seq):
    nq = seq // Q_BLOCK
    topk = min(TOPK_MAX, seq // 4)
    nch = seq // DSA_CH
    blk = lambda b, i: (b * nq + i, 0)
    full = lambda b, i: (b, 0)
    return pl.pallas_call(
        functools.partial(_dsa_prompt_kernel, topk=topk), grid=(batch, nq),
        in_specs=[pl.BlockSpec((Q_BLOCK, W_A), blk),
                  pl.BlockSpec((Q_BLOCK, 512), blk),
                  pl.BlockSpec((Q_BLOCK, LANES), blk),
                  pl.BlockSpec((seq, W_A), full),
                  pl.BlockSpec((nch, W_A, DSA_CH), lambda b, i: (b, 0, 0)),
                  pl.BlockSpec((seq, LANES), full)],
        out_specs=pl.BlockSpec((Q_BLOCK, W_A), blk),
        out_shape=jax.ShapeDtypeStruct((batch * seq, W_A), BF16),
        scratch_shapes=[pltpu.VMEM((nch, DSA_CH, LANES), I32),
                        pltpu.VMEM((nch, DSA_CH, LANES), F32),
                        pltpu.VMEM((H_A // 2, LANES, Q_BLOCK), F32)],
        compiler_params=_cparams(("arbitrary", "arbitrary")),
        name="dsa_prompt")(qa, qi, kiwi, ka_bf, va_t, kiwi_bf)


def _swish(g):
    return g * (1.0 / (1.0 + jnp.exp(-g)))


def _ret_prompt_kernel(q_ref, k_ref, v_ref, g_ref, dmask_ref, qdec_ref, kdec_ref, cdec_ref,
                       gret_ref, o_ref, sout_ref, s_scr, o_scr):
    c = pl.program_id(1)

    @pl.when(c == 0)
    def _():
        s_scr[...] = jnp.zeros_like(s_scr)

    q = q_ref[...]
    k = k_ref[...]
    v = v_ref[...]
    kd_t = (k * kdec_ref[...]).T.astype(BF16)
    for h in range(H_B):
        sl = slice(h * HEAD_DIM, (h + 1) * HEAD_DIM)
        qh = q[:, sl].astype(BF16)
        kh = k[:, sl].astype(BF16)
        vh = v[:, sl].astype(BF16)
        att = lax.dot_general(qh, kh, NT, preferred_element_type=F32) * dmask_ref[h]
        s_h = s_scr[h]
        o = (jnp.dot(att.astype(BF16), vh, preferred_element_type=F32)
             + jnp.dot(qh, s_h.astype(BF16), preferred_element_type=F32) * qdec_ref[:, sl])
        s_scr[h] = s_h * cdec_ref[h] + jnp.dot(kd_t[sl, :], vh, preferred_element_type=F32)
        mu = jnp.mean(o, axis=-1, keepdims=True)
        var = jnp.mean(jnp.square(o - mu), axis=-1, keepdims=True)
        on = (o - mu) * lax.rsqrt(var + EPS)
        o_scr[:, sl] = on * gret_ref[:, sl] * _swish(g_ref[:, sl])
    o_ref[...] = o_scr[...].astype(BF16)

    @pl.when(c == pl.num_programs(1) - 1)
    def _():
        sout_ref[...] = s_scr[...]


def _log_gamma():
    return jnp.log1p(-jnp.exp2(-5.0 - jnp.arange(H_B, dtype=F32)))


def _ret_tables(chunk):
    lg = _log_gamma()
    i = jnp.arange(chunk, dtype=F32)
    diff = i[:, None] - i[None, :]
    dmask = jnp.where(diff[None] >= 0, jnp.exp(jnp.maximum(diff, 0.0)[None] * lg[:, None, None]), 0.0)
    q_dec = jnp.exp((i[:, None] + 1.0) * lg[None, :])
    k_dec = jnp.exp((chunk - 1.0 - i)[:, None] * lg[None, :])
    c_dec = jnp.exp(chunk * lg)
    return dmask, q_dec, k_dec, c_dec


def _ret_prompt_call(qb, kb, vb, gb, g_ret, batch, seq):
    chunk = RET_CHUNK
    n = seq // chunk
    dmask, q_dec, k_dec, c_dec = _ret_tables(chunk)
    qdec = jnp.repeat(q_dec, HEAD_DIM, axis=1)
    kdec = jnp.repeat(k_dec, HEAD_DIM, axis=1)
    cdec = jnp.broadcast_to(c_dec[:, None, None], (H_B, HEAD_DIM, HEAD_DIM))
    blk = lambda b, c: (b * n + c, 0)
    c2 = lambda b, c: (0, 0)
    c3 = lambda b, c: (0, 0, 0)
    return pl.pallas_call(
        _ret_prompt_kernel, grid=(batch, n),
        in_specs=[pl.BlockSpec((chunk, W_B), blk)] * 4 + [
            pl.BlockSpec((H_B, chunk, chunk), c3),
            pl.BlockSpec((chunk, W_B), c2),
            pl.BlockSpec((chunk, W_B), c2),
            pl.BlockSpec((H_B, HEAD_DIM, HEAD_DIM), c3),
            pl.BlockSpec((1, W_B), c2)],
        out_specs=(pl.BlockSpec((chunk, W_B), blk),
                   pl.BlockSpec((None, H_B, HEAD_DIM, HEAD_DIM), lambda b, c: (b, 0, 0, 0))),
        out_shape=(jax.ShapeDtypeStruct((batch * seq, W_B), BF16),
                   jax.ShapeDtypeStruct((batch, H_B, HEAD_DIM, HEAD_DIM), F32)),
        scratch_shapes=[pltpu.VMEM((H_B, HEAD_DIM, HEAD_DIM), F32),
                        pltpu.VMEM((chunk, W_B), F32)],
        compiler_params=_cparams(("arbitrary", "arbitrary")),
        name="ret_prompt")(qb, kb, vb, gb, dmask, qdec, kdec, cdec, g_ret.reshape(1, W_B))


FF_CHUNK = 1024


def _outffn_kernel(a_ref, b_ref, x_ref, wo_ref, gpost_ref, gfpre_ref, gfpost_ref,
                   wup_ref, wdn_ref, y_ref):
    m = (jnp.dot(a_ref[...], wo_ref[0:W_A, :], preferred_element_type=F32)
         + jnp.dot(b_ref[...], wo_ref[W_A:D_MODEL, :], preferred_element_type=F32))
    x1 = x_ref[...] + _rms(m, gpost_ref[...])
    hn = _rms(x1, gfpre_ref[...]).astype(BF16)
    f = jnp.zeros(x1.shape, F32)
    for c in range(D_FF // FF_CHUNK):
        cs = slice(c * FF_CHUNK, (c + 1) * FF_CHUNK)
        u = jnp.dot(hn, wup_ref[:, cs], preferred_element_type=F32)
        u = jnp.square(jnp.maximum(u, 0.0)).astype(BF16)
        f = f + jnp.dot(u, wdn_ref[cs, :], preferred_element_type=F32)
    y_ref[...] = x1 + _rms(f, gfpost_ref[...])


def _outffn_call(a, b, x, wo, gpost, gfpre, gfpost, wup, wdn, tm):
    m = x.shape[0]
    row = lambda i: (i, 0)
    const = lambda i: (0, 0)
    once = dict(pipeline_mode=pl.Buffered(1))
    return pl.pallas_call(
        _outffn_kernel, grid=(m // tm,),
        in_specs=[pl.BlockSpec((tm, W_A), row), pl.BlockSpec((tm, W_B), row),
                  pl.BlockSpec((tm, D_MODEL), row),
                  pl.BlockSpec((D_MODEL, D_MODEL), const, **once),
                  pl.BlockSpec((1, D_MODEL), const), pl.BlockSpec((1, D_MODEL), const),
                  pl.BlockSpec((1, D_MODEL), const),
                  pl.BlockSpec((D_MODEL, D_FF), const, **once),
                  pl.BlockSpec((D_FF, D_MODEL), const, **once)],
        out_specs=pl.BlockSpec((tm, D_MODEL), row),
        out_shape=jax.ShapeDtypeStruct((m, D_MODEL), F32),
        compiler_params=_cparams(("arbitrary",)),
        name="outffn")(a, b, x, wo, gpost.reshape(1, -1), gfpre.reshape(1, -1),
                       gfpost.reshape(1, -1), wup, wdn)


def _dsa_sample_scores_kernel(pt_ref, *refs, n_pages):
    pages = refs[:n_pages]
    qi_ref, wi_ref, kinew_ref, o_ref = refs[n_pages:]
    qi = qi_ref[...]
    wi = wi_ref[...]

    def weighted(d):
        r = jnp.sum(wi * jnp.maximum(d, 0.0), axis=0, keepdims=True)
        return jnp.where(r == 0.0, 0.0, r)

    for p in range(n_pages):
        d = jnp.dot(qi, pages[p][...].astype(BF16), preferred_element_type=F32)
        o_ref[:, p * LANES:(p + 1) * LANES] = weighted(d)
    knew = kinew_ref[...].astype(BF16).astype(F32)
    d_new = jnp.sum(qi.astype(F32) * knew, axis=1, keepdims=True)
    lane = lax.broadcasted_iota(I32, (1, LANES), 1)
    o_ref[:, n_pages * LANES:(n_pages + 1) * LANES] = jnp.where(lane == 0, weighted(d_new), -jnp.inf)


def _dsa_sample_scores_call(page_table, pool_ki_t, layer, qi16, wi16, ki_new):
    nb, n_pages = page_table.shape
    ncol = (n_pages + 1) * LANES
    per = lambda b, pt: (b, 0, 0)
    page = lambda p: pl.BlockSpec((None, None, D_IDX, PAGE_SIZE),
                                  lambda b, pt: (layer, pt[b, p], 0, 0))
    return pl.pallas_call(
        functools.partial(_dsa_sample_scores_kernel, n_pages=n_pages),
        grid_spec=pltpu.PrefetchScalarGridSpec(
            num_scalar_prefetch=1, grid=(nb,),
            in_specs=[page(p) for p in range(n_pages)] + [
                pl.BlockSpec((None, 16, D_IDX), per),
                pl.BlockSpec((None, 16, 1), per),
                pl.BlockSpec((None, 1, D_IDX), per)],
            out_specs=pl.BlockSpec((None, 1, ncol), per)),
        out_shape=jax.ShapeDtypeStruct((nb, 1, ncol), F32),
        compiler_params=_cparams(("arbitrary",)),
        name="dsa_sample_scores")(page_table, *([pool_ki_t] * n_pages), qi16, wi16, ki_new)


def _dsa_sample_select_kernel(sc_ref, sel_ref, key_scr, *, topk, n_valid):
    ncol = sc_ref.shape[1]
    for p in range(ncol // LANES):
        rows = slice(p * LANES, (p + 1) * LANES)
        key_scr[0, rows, :] = _sort_key(sc_ref[:, rows].T)
    thr, cut = _topk_threshold(key_scr, 1, ncol, topk)
    krow = lax.broadcasted_iota(I32, (LANES, LANES), 0)
    for p in range(ncol // LANES):
        rows = slice(p * LANES, (p + 1) * LANES)
        k = key_scr[0, rows, :]
        kpos = krow + p * LANES
        sel = ((k > thr) | ((k == thr) & (kpos < cut))) & (kpos < n_valid)
        sel_ref[:, rows] = jnp.where(sel, 1.0, 0.0).T


def _dsa_sample_select_call(sc, topk, n_valid):
    nb, ncol = sc.shape
    return pl.pallas_call(
        functools.partial(_dsa_sample_select_kernel, topk=topk, n_valid=n_valid),
        out_shape=jax.ShapeDtypeStruct((nb, ncol), F32),
        scratch_shapes=[pltpu.VMEM((1, ncol, nb), I32)],
        compiler_params=pltpu.CompilerParams(vmem_limit_bytes=VMEM_LIMIT),
        name="dsa_sample_select")(sc)


def _dsa_sample_attn_kernel(pt_ref, *refs, n_pages):
    kpages = refs[:n_pages]
    vpages = refs[n_pages:2 * n_pages]
    q_ref, knew_ref, vnew_ref, sel_ref, o_ref = refs[2 * n_pages:]
    rows = lax.broadcasted_iota(I32, (16, W_A), 0)
    lanes = lax.broadcasted_iota(I32, (16, W_A), 1)
    own_head = lax.shift_right_logical(lanes, 6) == rows
    q_bd = jnp.where(own_head, jnp.broadcast_to(q_ref[...], (16, W_A)), 0.0).astype(BF16)
    first_pos = lax.broadcasted_iota(I32, (PAGE_SIZE, W_A), 0) == 0

    def new_tile(ref):
        return jnp.where(first_pos, jnp.broadcast_to(ref[...], (PAGE_SIZE, W_A)), 0.0).astype(BF16)

    def page_t(ref):
        return ref[...].reshape(W_A, PAGE_SIZE).astype(BF16)

    s_all = []
    for p in range(n_pages + 1):
        if p < n_pages:
            s = jnp.dot(q_bd, page_t(kpages[p]), preferred_element_type=F32)
        else:
            s = lax.dot_general(q_bd, new_tile(knew_ref), NT, preferred_element_type=F32)
        s_all.append(jnp.where(sel_ref[:, p * LANES:(p + 1) * LANES] > 0.5, s, NEG))
    m = s_all[0]
    for s in s_all[1:]:
        m = jnp.maximum(m, s)
    m = jnp.max(m, axis=1, keepdims=True)
    p_all = [jnp.exp(s - m) for s in s_all]
    l = p_all[0]
    for pp in p_all[1:]:
        l = l + pp
    inv = 1.0 / jnp.sum(l, axis=1, keepdims=True)
    acc = jnp.zeros((16, W_A), F32)
    for p in range(n_pages + 1):
        pn = (p_all[p] * inv).astype(BF16)
        if p < n_pages:
            acc = acc + lax.dot_general(pn, page_t(vpages[p]), NT, preferred_element_type=F32)
        else:
            acc = acc + jnp.dot(pn, new_tile(vnew_ref), preferred_element_type=F32)
    o_ref[...] = jnp.sum(jnp.where(own_head, acc, 0.0), axis=0, keepdims=True)


def _dsa_sample_attn_call(page_table, pool_k_t, pool_v_t, layer, q, k_new, v_new, sel):
    nb, n_pages = page_table.shape
    per = lambda b, pt: (b, 0, 0)
    page = lambda p: pl.BlockSpec((None, None, H_A, HEAD_DIM, PAGE_SIZE),
                                  lambda b, pt: (layer, pt[b, p], 0, 0, 0))
    row_blk = pl.BlockSpec((None, 1, W_A), per)
    return pl.pallas_call(
        functools.partial(_dsa_sample_attn_kernel, n_pages=n_pages),
        grid_spec=pltpu.PrefetchScalarGridSpec(
            num_scalar_prefetch=1, grid=(nb,),
            in_specs=[page(p) for p in range(n_pages)] * 2 + [
                row_blk, row_blk, row_blk,
                pl.BlockSpec((None, 1, sel.shape[2]), per)],
            out_specs=row_blk),
        out_shape=jax.ShapeDtypeStruct((nb, 1, W_A), F32),
        compiler_params=_cparams(("arbitrary",)),
        name="dsa_sample_attn")(page_table, *([pool_k_t] * n_pages), *([pool_v_t] * n_pages),
                                q, k_new, v_new, sel)


def _ret_sample_kernel(q_ref, k_ref, v_ref, g_ref, s_ref, gret_ref, gam_ref, o_ref, sout_ref):
    v = v_ref[...]
    gam = gam_ref[...]

    def body(d, o):
        s_new = s_ref[d] * gam + k_ref[pl.ds(d, 1), :] * v
        sout_ref[d] = s_new
        return o + q_ref[pl.ds(d, 1), :] * s_new

    o = lax.fori_loop(0, HEAD_DIM, body, jnp.zeros(v.shape, F32))
    mu = jnp.mean(o, axis=0, keepdims=True)
    var = jnp.mean(jnp.square(o - mu), axis=0, keepdims=True)
    on = (o - mu) * lax.rsqrt(var + EPS)
    o_ref[...] = on * gret_ref[...] * _swish(g_ref[...])


def _ret_sample_call(qb, kb, vb, gb, state_t, layer, g_ret):
    nb = qb.shape[0]
    gam = jnp.exp(1.0 * _log_gamma())
    gam_b = jnp.broadcast_to(jnp.repeat(gam, HEAD_DIM)[:, None], (W_B, nb))
    gret_b = jnp.broadcast_to(g_ret[:, None], (W_B, nb))
    head = pl.BlockSpec((HEAD_DIM, nb), lambda h: (h, 0))
    o_t, s_new = pl.pallas_call(
        _ret_sample_kernel, grid=(H_B,),
        in_specs=[head, head, head, head,
                  pl.BlockSpec((None, None, HEAD_DIM, HEAD_DIM, nb), lambda h: (layer, h, 0, 0, 0)),
                  head, head],
        out_specs=(head, pl.BlockSpec((None, HEAD_DIM, HEAD_DIM, nb), lambda h: (h, 0, 0, 0))),
        out_shape=(jax.ShapeDtypeStruct((W_B, nb), F32),
                   jax.ShapeDtypeStruct((H_B, HEAD_DIM, HEAD_DIM, nb), F32)),
        compiler_params=_cparams(("arbitrary",)),
        name="ret_sample")(qb.T, kb.T, vb.T, gb.T, state_t, gret_b, gam_b)
    return o_t.T.astype(BF16), s_new


def _pack_w_in_t(w_t):
    d = w_t.shape[1]
    splits = np.cumsum(COL_SIZES)[:-1].tolist()
    qa, ka, va, qi, ki, wi, qb, kb, vb, gb = jnp.split(w_t, splits, axis=0)
    qi_pad = jnp.pad(qi.reshape(H_IDX, D_IDX, d), ((0, 0), (0, LANES - D_IDX), (0, 0))).reshape(-1, d)
    kiwi = jnp.concatenate([ki, wi, jnp.zeros((LANES - D_IDX - H_IDX, d), w_t.dtype)], axis=0)
    return jnp.concatenate([qa, ka, va, qi_pad, kiwi, qb, kb, vb, gb], axis=0).astype(BF16)


def _rope_tables(pos):
    half = HEAD_DIM // 2
    inv = ROPE_THETA ** (-jnp.arange(half, dtype=F32) / half)
    ang = pos[:, None] * inv[None, :]
    c = jnp.cos(ang)
    s = jnp.sin(ang)
    return jnp.concatenate([c, c, c, c], axis=1), jnp.concatenate([-s, s, -s, s], axis=1)


def kernel(x_prompt, x_sample, cache_k, cache_v, cache_kidx, state_ret, page_table, w_in, w_out,
           g_ret, g_mix_pre, g_mix_post, g_ffn_pre, g_ffn_post, w_up, w_down):
    batch, seq, _ = x_prompt.shape
    nb, dec_seq, _ = x_sample.shape
    depth = w_in.shape[0]
    n_pages = page_table.shape[1]
    past = n_pages * PAGE_SIZE
    assert dec_seq == 1 and seq % DSA_CH == 0 and nb == LANES
    tm_p = DSA_CH

    cos_p, sin_p = _rope_tables(jnp.arange(seq, dtype=F32))
    cos_s, sin_s = _rope_tables(past + jnp.arange(dec_seq, dtype=F32))
    cos_s = jnp.broadcast_to(cos_s, (nb, LANES))
    sin_s = jnp.broadcast_to(sin_s, (nb, LANES))
    pool_k_t = jnp.transpose(cache_k, (0, 1, 3, 4, 2))
    pool_v_t = jnp.transpose(cache_v, (0, 1, 3, 4, 2))
    pool_ki_t = jnp.swapaxes(cache_kidx, 2, 3)
    state_t = jnp.transpose(state_ret, (0, 2, 3, 4, 1))
    w_in_t = jnp.transpose(w_in, (2, 0, 1))
    wo_bf = w_out.astype(BF16)
    wup_bf = w_up.astype(BF16)
    wdn_bf = w_down.astype(BF16)
    topk_s = min(TOPK_MAX, (past + dec_seq) // 4)

    yp = x_prompt.reshape(batch * seq, D_MODEL)
    ys = x_sample.reshape(nb, D_MODEL)
    kp, vp, kip, sp, kss, vss, kis, ss = [], [], [], [], [], [], [], []
    for l in range(depth):
        w_pack = _pack_w_in_t(w_in_t[:, l, :])
        wo, wup, wdn = wo_bf[l], wup_bf[l], wdn_bf[l]
        gpre = g_mix_pre[l].reshape(1, D_MODEL)

        (qa, ka, ka_bf, va, va_t, qi, kiwi, kiwi_bf, ki, qb, kb, vb, gb) = _proj_call(
            yp, gpre, w_pack, cos_p, sin_p, tm_p)
        a = _dsa_prompt_call(qa, qi, kiwi, ka_bf, va_t, kiwi_bf, batch, seq)
        bmix, s_new = _ret_prompt_call(qb, kb, vb, gb, g_ret[l], batch, seq)
        yp = _outffn_call(a, bmix, yp, wo, g_mix_post[l], g_ffn_pre[l], g_ffn_post[l], wup, wdn, 256)
        kp.append(ka); vp.append(va); kip.append(ki); sp.append(s_new)

        (qa, ka, ka_bf, va, va_t, qi, kiwi, kiwi_bf, ki, qb, kb, vb, gb) = _proj_call(
            ys, gpre, w_pack, cos_s, sin_s, nb)
        qi16 = jnp.pad(qi.reshape(nb, H_IDX, LANES)[:, :, :D_IDX], ((0, 0), (0, 16 - H_IDX), (0, 0)))
        wi16 = jnp.pad(kiwi[:, D_IDX:D_IDX + H_IDX], ((0, 0), (0, 16 - H_IDX)))[:, :, None]
        sc = _dsa_sample_scores_call(page_table, pool_ki_t, l, qi16, wi16, ki.reshape(nb, 1, D_IDX))
        sel = _dsa_sample_select_call(sc.reshape(nb, -1), topk_s, past + dec_seq)
        a = _dsa_sample_attn_call(page_table, pool_k_t, pool_v_t, l,
                                  qa.astype(F32).reshape(nb, 1, W_A),
                                  ka.reshape(nb, 1, W_A), va.reshape(nb, 1, W_A),
                                  sel.reshape(nb, 1, -1))
        bmix, s_new = _ret_sample_call(qb, kb, vb, gb, state_t, l, g_ret[l])
        ys = _outffn_call(a.reshape(nb, W_A).astype(BF16), bmix, ys, wo, g_mix_post[l], g_ffn_pre[l],
                          g_ffn_post[l], wup, wdn, nb)
        kss.append(ka); vss.append(va); kis.append(ki); ss.append(s_new)

    def heads(xs, lead):
        return jnp.stack(xs).reshape((depth,) + lead + (H_A, HEAD_DIM))

    ret_s = jnp.transpose(jnp.stack(ss), (0, 4, 1, 2, 3))
    return (yp.reshape(batch, seq, D_MODEL), ys.reshape(nb, dec_seq, D_MODEL),
            heads(kp, (batch, seq)), heads(vp, (batch, seq)),
            jnp.stack(kip).reshape(depth, batch, seq, D_IDX), jnp.stack(sp),
            heads(kss, (nb, dec_seq)), heads(vss, (nb, dec_seq)),
            jnp.stack(kis).reshape(depth, nb, dec_seq, D_IDX), ret_s)
```

```python
import functools

import numpy as np
import jax
import jax.numpy as jnp
from jax import lax
from jax.experimental import pallas as pl
from jax.experimental.pallas import tpu as pltpu

F32 = jnp.float32
BF16 = jnp.bfloat16
I32 = jnp.int32

D_MODEL = 1024
HEAD_DIM = 64
H_A = 8
H_B = 8
H_IDX = 4
D_IDX = 64
W_A = H_A * HEAD_DIM
W_B = H_B * HEAD_DIM
D_FF = 4 * D_MODEL
TOPK_MAX = 256
PAGE_SIZE = 128
ROPE_THETA = 10000.0
EPS = 1e-6
Q_BLOCK = 128
RET_CHUNK = 128
COL_SIZES = (W_A, W_A, W_A, H_IDX * D_IDX, D_IDX, H_IDX, W_B, W_B, W_B, W_B)

LANES = 128
NEG = -1e30
INT_MIN = -2147483648
VMEM_LIMIT = 48 * 1024 * 1024

LOG2E = 1.4426950408889634
VT_ROWS = 144
VT_ALL = (H_A // 2) * VT_ROWS

NT = (((1,), (1,)), ((), ()))

C_QA, C_KA, C_VA, C_QI, C_KIWI, C_QB, C_KB, C_VB, C_GB = (
    0, 512, 1024, 1536, 2048, 2176, 2688, 3200, 3712)
N_PACK = 4224


def _cparams(sem):
    return pltpu.CompilerParams(dimension_semantics=sem, vmem_limit_bytes=VMEM_LIMIT)


def _rms(x, g):
    return x * lax.rsqrt(jnp.mean(x * x, axis=-1, keepdims=True) + EPS) * g


def _sort_key(x):
    bits = pltpu.bitcast(x, I32)
    return bits ^ ((bits >> 31) & 0x7FFFFFFF)


def _proj_kernel(x_ref, g_ref, w_ref, cos_ref, sin_ref,
                 qa_ref, ka_ref, kabf_ref, va_ref, vat_ref, qi_ref, kiwi_ref, kiwibf_ref,
                 ki_ref, qb_ref, kb_ref, vb_ref, gb_ref):
    x = x_ref[...]
    h = _rms(x, g_ref[...]).astype(BF16)
    cos = cos_ref[...]
    sin = sin_ref[...]
    lane = lax.broadcasted_iota(I32, cos.shape, 1)
    first_half = (lane % 64) < 32

    def rope(z, c, s):
        sw = jnp.where(first_half, pltpu.roll(z, 96, 1), pltpu.roll(z, 32, 1))
        return z * c + sw * s

    def proj(c0, n):
        return lax.dot_general(h, w_ref[c0:c0 + n, :], NT, preferred_element_type=F32)

    def roped(c0, scale):
        z = proj(c0, 512)
        parts = []
        for j in range(4):
            r = rope(z[:, j * LANES:(j + 1) * LANES], cos, sin)
            parts.append(r if scale is None else r * scale)
        return parts

    qa = roped(C_QA, HEAD_DIM ** -0.5 * LOG2E)
    for j in range(4):
        qa_ref[:, j * LANES:(j + 1) * LANES] = qa[j].astype(BF16)
    ka = roped(C_KA, None)
    for j in range(4):
        ka_ref[:, j * LANES:(j + 1) * LANES] = ka[j]
        kabf_ref[:, j * LANES:(j + 1) * LANES] = ka[j].astype(BF16)
    va = proj(C_VA, 512)
    va_ref[...] = va
    va_t = va.T
    ones = jnp.ones((VT_ROWS - LANES, va_t.shape[1]), BF16)
    for p in range(H_A // 2):
        vat_ref[p * VT_ROWS:p * VT_ROWS + LANES, :] = va_t[p * LANES:(p + 1) * LANES, :].astype(BF16)
        vat_ref[p * VT_ROWS + LANES:(p + 1) * VT_ROWS, :] = ones
    qi = roped(C_QI, None)
    for j in range(4):
        qi_ref[:, j * LANES:(j + 1) * LANES] = qi[j].astype(BF16)
    zk = proj(C_KIWI, LANES)
    is_key = lane < 64
    kiwi = rope(zk, jnp.where(is_key, cos, (H_IDX ** -0.5) * (D_IDX ** -0.5)),
                jnp.where(is_key, sin, 0.0))
    kiwi_ref[...] = kiwi
    kiwibf_ref[...] = kiwi.astype(BF16)
    ki_ref[...] = kiwi[:, 0:64]
    qb = roped(C_QB, None)
    for j in range(4):
        qb_ref[:, j * LANES:(j + 1) * LANES] = qb[j]
    kb = roped(C_KB, HEAD_DIM ** -0.5)
    for j in range(4):
        kb_ref[:, j * LANES:(j + 1) * LANES] = kb[j]
    vb_ref[...] = proj(C_VB, 512)
    gb_ref[...] = proj(C_GB, 512)


def _proj_call(x, g, w_pack_t, cos_t, sin_t, tm):
    m = x.shape[0]
    n_t = cos_t.shape[0] // tm
    row = lambda i: (i, 0)
    const = lambda i: (0, 0)
    tab = lambda i: (i % n_t, 0)
    wide = lambda dt: jax.ShapeDtypeStruct((m, 512), dt)
    out_shape = (wide(BF16), wide(F32), wide(BF16), wide(F32),
                 jax.ShapeDtypeStruct((m // tm, VT_ALL, tm), BF16), wide(BF16),
                 jax.ShapeDtypeStruct((m, LANES), F32), jax.ShapeDtypeStruct((m, LANES), BF16),
                 jax.ShapeDtypeStruct((m, 64), F32), wide(F32), wide(F32), wide(F32), wide(F32))
    out_specs = tuple(
        pl.BlockSpec((None, VT_ALL, tm), lambda i: (i, 0, 0)) if len(s.shape) == 3
        else pl.BlockSpec((tm, s.shape[1]), row) for s in out_shape)
    return pl.pallas_call(
        _proj_kernel, grid=(m // tm,),
        in_specs=[pl.BlockSpec((tm, D_MODEL), row),
                  pl.BlockSpec((1, D_MODEL), const),
                  pl.BlockSpec((N_PACK, D_MODEL), const, pipeline_mode=pl.Buffered(1)),
                  pl.BlockSpec((tm, LANES), tab),
                  pl.BlockSpec((tm, LANES), tab)],
        out_specs=out_specs, out_shape=out_shape,
        compiler_params=_cparams(("arbitrary",)), name="proj")(x, g, w_pack_t, cos_t, sin_t)


I16 = jnp.int16
I16_MIN = -32768
I16_MAX = 32767
COUNT_ROWS = 32


def _store_split_keys(key, idx, key_scr, hi_scr, lo_scr):
    key_scr[idx] = key
    hi_scr[idx] = (key >> 16).astype(I16)
    lo_scr[idx] = ((key & 0xFFFF) - 32768).astype(I16)


def _count16(scr, nc, ch, pred):
    one = jnp.ones((), I16)
    zero = jnp.zeros((), I16)

    def body(c, acc):
        m = jnp.where(pred(scr[c]), one, zero)
        for r in range(ch // COUNT_ROWS):
            acc = acc + m[r * COUNT_ROWS:(r + 1) * COUNT_ROWS]
        return acc

    acc = lax.fori_loop(0, nc, body, jnp.zeros((COUNT_ROWS, LANES), I16))
    return jnp.sum(acc.astype(F32), axis=0, keepdims=True)


def _search16(scr, nc, ch, target):
    cnt0 = _count16(scr, nc, ch, lambda k: k >= jnp.zeros((), I16))
    lo0 = jnp.where(cnt0 >= target, 0, I16_MIN)

    def bit_body(j, lo):
        cand = lo + jnp.left_shift(jnp.int32(1), 14 - j)
        c16 = cand.astype(I16)
        cnt = _count16(scr, nc, ch, lambda k: k >= c16)
        return jnp.where(cnt >= target, cand, lo)

    return lax.fori_loop(0, 15, bit_body, lo0)


def _topk_threshold(key_scr, hi_scr, lo_scr, nc, ch, topk):
    assert ch >= topk and ch % COUNT_ROWS == 0
    kf = float(topk)
    t_hi = _search16(hi_scr, nc, ch, kf)
    t_hi16 = t_hi.astype(I16)
    need_lo = kf - _count16(hi_scr, nc, ch, lambda k: k > t_hi16)

    def bucket_body(c, carry):
        lo_scr[c] = jnp.where(hi_scr[c] == t_hi16, lo_scr[c], jnp.full((), I16_MIN, I16))
        return carry

    lax.fori_loop(0, nc, bucket_body, 0)
    t_lo = _search16(lo_scr, nc, ch, need_lo)
    t_lo16 = t_lo.astype(I16)
    need_tie = need_lo - _count16(lo_scr, nc, ch, lambda k: k > t_lo16)
    thr = (t_hi << 16) | (t_lo + 32768)

    n_tied = _count16(lo_scr, nc, ch, lambda k: k == t_lo16)
    krow = lax.broadcasted_iota(I32, (ch, LANES), 0)

    def tie_search():
        def tie_prep(c, carry):
            lo_scr[c] = jnp.where(key_scr[c] == thr, krow + c * ch, I16_MAX).astype(I16)
            return carry

        lax.fori_loop(0, nc, tie_prep, 0)

        def tie_body(j, cut):
            cand = cut + jnp.left_shift(jnp.int32(1), 12 - j)
            c16 = cand.astype(I16)
            f = _count16(lo_scr, nc, ch, lambda k: k < c16)
            return jnp.where(f <= need_tie, cand, cut)

        return lax.fori_loop(0, 13, tie_body, jnp.zeros((1, LANES), I32))

    cut = lax.cond(jnp.max(n_tied - need_tie) > 0.0, tie_search,
                   lambda: jnp.full((1, LANES), I16_MAX, I32))
    return thr, cut


DSA_CH = 512


def _dsa_prompt_kernel(qa_ref, qi_ref, kiwi_ref, ka_ref, vat_ref, kibf_ref, o_ref,
                       key_scr, hi_scr, lo_scr, bias_scr, acc_scr, qcat_scr, qicat_scr,
                       sa_scr, sb_scr, *, topk):
    ch = DSA_CH
    n_pair = H_A // 2
    i = pl.program_id(1)
    t0 = i * Q_BLOCK
    nc = lax.div(t0 + Q_BLOCK + ch - 1, ch)
    qpos = t0 + lax.broadcasted_iota(I32, (ch, LANES), 1)
    krow = lax.broadcasted_iota(I32, (ch, LANES), 0)
    wi_t = kiwi_ref[...].T

    lane_lo = lax.broadcasted_iota(I32, (Q_BLOCK, LANES), 1) < HEAD_DIM
    for p in range(n_pair):
        qp = qa_ref[:, p * LANES:(p + 1) * LANES]
        zero = jnp.zeros_like(qp)
        qcat_scr[p, 0:Q_BLOCK, :] = jnp.where(lane_lo, qp, zero)
        qcat_scr[p, Q_BLOCK:2 * Q_BLOCK, :] = jnp.where(lane_lo, zero, qp)
    for g in range(H_IDX // 2):
        qicat_scr[g, 0:Q_BLOCK, :] = qi_ref[:, (2 * g) * LANES:(2 * g + 1) * LANES]
        qicat_scr[g, Q_BLOCK:2 * Q_BLOCK, :] = qi_ref[:, (2 * g + 1) * LANES:(2 * g + 2) * LANES]

    def score_body(c, carry):
        off = pl.multiple_of(c * ch, ch)
        kic = kibf_ref[pl.ds(off, ch), :]
        acc = jnp.zeros((ch, LANES), F32)
        for g in range(H_IDX // 2):
            d = lax.dot_general(kic, qicat_scr[g], NT, preferred_element_type=F32)
            acc = (acc + wi_t[64 + 2 * g:65 + 2 * g, :] * jnp.maximum(d[:, 0:LANES], 0.0)
                   + wi_t[65 + 2 * g:66 + 2 * g, :] * jnp.maximum(d[:, LANES:2 * LANES], 0.0))
        acc = jnp.where(acc == 0.0, 0.0, acc)
        sc = jnp.where(krow + c * ch <= qpos, acc, -jnp.inf)
        _store_split_keys(_sort_key(sc), sel_index(c), key_scr, hi_scr, lo_scr)
        return carry

    def sel_index(c):
        return (lax.shift_right_logical(c, 1), pl.ds(pl.multiple_of((c & 1) * ch, ch), ch))

    lax.fori_loop(0, nc, score_body, 0)

    odd = (nc & 1) == 1

    @pl.when(odd)
    def _():
        idx = sel_index(nc)
        key_scr[idx] = jnp.full((ch, LANES), INT_MIN, I32)
        hi_scr[idx] = jnp.full((ch, LANES), I16_MIN, I16)
        lo_scr[idx] = jnp.full((ch, LANES), I16_MIN, I16)
        bias_scr[nc] = jnp.full((ch, LANES), NEG, F32)

    n2 = lax.shift_right_logical(nc + 1, 1)
    thr, cut = _topk_threshold(key_scr, hi_scr, lo_scr, n2, 2 * ch, topk)

    def bias_body(c, carry):
        k = key_scr[sel_index(c)]
        kpos = krow + c * ch
        sel = ((k > thr) | ((k == thr) & (kpos < cut))) & (kpos <= qpos)
        bias_scr[c] = jnp.where(sel, 0.0, NEG)
        return carry

    lax.fori_loop(0, nc, bias_body, 0)

    row_lo = lax.broadcasted_iota(I32, (LANES, Q_BLOCK), 0) < HEAD_DIM
    acc_scr[...] = jnp.zeros_like(acc_scr)
    last_chunk = ka_ref.shape[0] // ch - 1

    def qk_scores(c, p):
        off = pl.multiple_of(jnp.minimum(c, last_chunk) * ch, ch)
        return lax.dot_general(ka_ref[pl.ds(off, ch), p * LANES:(p + 1) * LANES], qcat_scr[p], NT,
                               preferred_element_type=F32)

    for p in range(n_pair):
        sa_scr[p] = qk_scores(0, p)

    def half_step(c, cur_scr, nxt_scr, carry):
        b = bias_scr[c]
        new = []
        for p in range(n_pair):
            nxt_scr[p] = qk_scores(c + 1, p)
            s = cur_scr[p]
            alphas, pts, ms = [], [], []
            for e in range(2):
                m = carry[4 * p + 2 * e]
                sb = s[:, e * LANES:(e + 1) * LANES] + b
                m_new = jnp.maximum(m, jnp.max(sb, axis=0, keepdims=True))
                alphas.append(jnp.exp2(m - m_new))
                pts.append(jnp.exp2(sb - m_new).astype(BF16))
                ms.append(m_new)
            pv = jnp.dot(vat_ref[c, p * VT_ROWS:(p + 1) * VT_ROWS, :],
                         jnp.concatenate(pts, axis=1), preferred_element_type=F32)
            for e in range(2):
                l = carry[4 * p + 2 * e + 1]
                new += [ms[e], alphas[e] * l + pv[LANES:LANES + 1, e * LANES:(e + 1) * LANES]]
            acc_scr[p] = (acc_scr[p] * jnp.where(row_lo, alphas[0], alphas[1])
                          + jnp.where(row_lo, pv[0:LANES, 0:LANES], pv[0:LANES, LANES:2 * LANES]))
        return tuple(new)

    def att_body(j, carry):
        carry = half_step(2 * j, sa_scr, sb_scr, carry)
        return half_step(2 * j + 1, sb_scr, sa_scr, carry)

    init = (jnp.full((1, LANES), NEG, F32), jnp.zeros((1, LANES), F32)) * H_A
    stats = lax.fori_loop(0, n2, att_body, init)
    for p in range(n_pair):
        l_pair = jnp.where(row_lo, stats[4 * p + 1], stats[4 * p + 3])
        o_ref[:, p * LANES:(p + 1) * LANES] = (acc_scr[p] / l_pair).T.astype(BF16)


def _dsa_prompt_call(qa, qi, kiwi, ka_bf, va_t, kiwi_bf, batch, seq):
    nq = seq // Q_BLOCK
    topk = min(TOPK_MAX, seq // 4)
    nch = seq // DSA_CH
    blk = lambda b, i: (b * nq + i, 0)
    full = lambda b, i: (b, 0)
    return pl.pallas_call(
        functools.partial(_dsa_prompt_kernel, topk=topk), grid=(batch, nq),
        in_specs=[pl.BlockSpec((Q_BLOCK, W_A), blk),
                  pl.BlockSpec((Q_BLOCK, 512), blk),
                  pl.BlockSpec((Q_BLOCK, LANES), blk),
                  pl.BlockSpec((seq, W_A), full),
                  pl.BlockSpec((nch, VT_ALL, DSA_CH), lambda b, i: (b, 0, 0)),
                  pl.BlockSpec((seq, LANES), full)],
        out_specs=pl.BlockSpec((Q_BLOCK, W_A), blk),
        out_shape=jax.ShapeDtypeStruct((batch * seq, W_A), BF16),
        scratch_shapes=[pltpu.VMEM((nch // 2, 2 * DSA_CH, LANES), I32),
                        pltpu.VMEM((nch // 2, 2 * DSA_CH, LANES), I16),
                        pltpu.VMEM((nch // 2, 2 * DSA_CH, LANES), I16),
                        pltpu.VMEM((nch, DSA_CH, LANES), F32),
                        pltpu.VMEM((H_A // 2, LANES, Q_BLOCK), F32),
                        pltpu.VMEM((H_A // 2, 2 * Q_BLOCK, LANES), BF16),
                        pltpu.VMEM((H_IDX // 2, 2 * Q_BLOCK, LANES), BF16),
                        pltpu.VMEM((H_A // 2, DSA_CH, 2 * Q_BLOCK), F32),
                        pltpu.VMEM((H_A // 2, DSA_CH, 2 * Q_BLOCK), F32)],
        compiler_params=_cparams(("arbitrary", "arbitrary")),
        name="dsa_prompt")(qa, qi, kiwi, ka_bf, va_t, kiwi_bf)


def _swish(g):
    return g * (1.0 / (1.0 + jnp.exp(-g)))


def _ret_prompt_kernel(q_ref, k_ref, v_ref, g_ref, dmask_ref, qdec_ref, kdec_ref, cdec_ref,
                       gret_ref, o_ref, sout_ref, s_scr, o_scr):
    c = pl.program_id(1)

    @pl.when(c == 0)
    def _():
        s_scr[...] = jnp.zeros_like(s_scr)

    q = q_ref[...]
    k = k_ref[...]
    v = v_ref[...]
    kd_t = (k * kdec_ref[...]).T.astype(BF16)
    for h in range(H_B):
        sl = slice(h * HEAD_DIM, (h + 1) * HEAD_DIM)
        qh = q[:, sl].astype(BF16)
        kh = k[:, sl].astype(BF16)
        vh = v[:, sl].astype(BF16)
        att = lax.dot_general(qh, kh, NT, preferred_element_type=F32) * dmask_ref[h]
        s_h = s_scr[h]
        o = (jnp.dot(att.astype(BF16), vh, preferred_element_type=F32)
             + jnp.dot(qh, s_h.astype(BF16), preferred_element_type=F32) * qdec_ref[:, sl])
        s_scr[h] = s_h * cdec_ref[h] + jnp.dot(kd_t[sl, :], vh, preferred_element_type=F32)
        mu = jnp.mean(o, axis=-1, keepdims=True)
        var = jnp.mean(jnp.square(o - mu), axis=-1, keepdims=True)
        on = (o - mu) * lax.rsqrt(var + EPS)
        o_scr[:, sl] = on * gret_ref[:, sl] * _swish(g_ref[:, sl])
    o_ref[...] = o_scr[...].astype(BF16)

    @pl.when(c == pl.num_programs(1) - 1)
    def _():
        sout_ref[...] = s_scr[...]


def _log_gamma():
    return jnp.log1p(-jnp.exp2(-5.0 - jnp.arange(H_B, dtype=F32)))


def _ret_tables(chunk):
    lg = _log_gamma()
    i = jnp.arange(chunk, dtype=F32)
    diff = i[:, None] - i[None, :]
    dmask = jnp.where(diff[None] >= 0, jnp.exp(jnp.maximum(diff, 0.0)[None] * lg[:, None, None]), 0.0)
    q_dec = jnp.exp((i[:, None] + 1.0) * lg[None, :])
    k_dec = jnp.exp((chunk - 1.0 - i)[:, None] * lg[None, :])
    c_dec = jnp.exp(chunk * lg)
    return dmask, q_dec, k_dec, c_dec


def _ret_prompt_call(qb, kb, vb, gb, g_ret, batch, seq):
    chunk = RET_CHUNK
    n = seq // chunk
    dmask, q_dec, k_dec, c_dec = _ret_tables(chunk)
    qdec = jnp.repeat(q_dec, HEAD_DIM, axis=1)
    kdec = jnp.repeat(k_dec, HEAD_DIM, axis=1)
    cdec = jnp.broadcast_to(c_dec[:, None, None], (H_B, HEAD_DIM, HEAD_DIM))
    blk = lambda b, c: (b * n + c, 0)
    c2 = lambda b, c: (0, 0)
    c3 = lambda b, c: (0, 0, 0)
    return pl.pallas_call(
        _ret_prompt_kernel, grid=(batch, n),
        in_specs=[pl.BlockSpec((chunk, W_B), blk)] * 4 + [
            pl.BlockSpec((H_B, chunk, chunk), c3),
            pl.BlockSpec((chunk, W_B), c2),
            pl.BlockSpec((chunk, W_B), c2),
            pl.BlockSpec((H_B, HEAD_DIM, HEAD_DIM), c3),
            pl.BlockSpec((1, W_B), c2)],
        out_specs=(pl.BlockSpec((chunk, W_B), blk),
                   pl.BlockSpec((None, H_B, HEAD_DIM, HEAD_DIM), lambda b, c: (b, 0, 0, 0))),
        out_shape=(jax.ShapeDtypeStruct((batch * seq, W_B), BF16),
                   jax.ShapeDtypeStruct((batch, H_B, HEAD_DIM, HEAD_DIM), F32)),
        scratch_shapes=[pltpu.VMEM((H_B, HEAD_DIM, HEAD_DIM), F32),
                        pltpu.VMEM((chunk, W_B), F32)],
        compiler_params=_cparams(("arbitrary", "arbitrary")),
        name="ret_prompt")(qb, kb, vb, gb, dmask, qdec, kdec, cdec, g_ret.reshape(1, W_B))


FF_CHUNK = 1024


def _outffn_kernel(a_ref, b_ref, x_ref, wo_ref, gpost_ref, gfpre_ref, gfpost_ref,
                   wup_ref, wdn_ref, y_ref):
    m = (jnp.dot(a_ref[...], wo_ref[0:W_A, :], preferred_element_type=F32)
         + jnp.dot(b_ref[...], wo_ref[W_A:D_MODEL, :], preferred_element_type=F32))
    x1 = x_ref[...] + _rms(m, gpost_ref[...])
    hn = _rms(x1, gfpre_ref[...]).astype(BF16)
    f = jnp.zeros(x1.shape, F32)
    for c in range(D_FF // FF_CHUNK):
        cs = slice(c * FF_CHUNK, (c + 1) * FF_CHUNK)
        u = jnp.dot(hn, wup_ref[:, cs], preferred_element_type=F32)
        u = jnp.square(jnp.maximum(u, 0.0)).astype(BF16)
        f = f + jnp.dot(u, wdn_ref[cs, :], preferred_element_type=F32)
    y_ref[...] = x1 + _rms(f, gfpost_ref[...])


def _outffn_call(a, b, x, wo, gpost, gfpre, gfpost, wup, wdn, tm):
    m = x.shape[0]
    row = lambda i: (i, 0)
    const = lambda i: (0, 0)
    once = dict(pipeline_mode=pl.Buffered(1))
    return pl.pallas_call(
        _outffn_kernel, grid=(m // tm,),
        in_specs=[pl.BlockSpec((tm, W_A), row), pl.BlockSpec((tm, W_B), row),
                  pl.BlockSpec((tm, D_MODEL), row),
                  pl.BlockSpec((D_MODEL, D_MODEL), const, **once),
                  pl.BlockSpec((1, D_MODEL), const), pl.BlockSpec((1, D_MODEL), const),
                  pl.BlockSpec((1, D_MODEL), const),
                  pl.BlockSpec((D_MODEL, D_FF), const, **once),
                  pl.BlockSpec((D_FF, D_MODEL), const, **once)],
        out_specs=pl.BlockSpec((tm, D_MODEL), row),
        out_shape=jax.ShapeDtypeStruct((m, D_MODEL), F32),
        compiler_params=_cparams(("arbitrary",)),
        name="outffn")(a, b, x, wo, gpost.reshape(1, -1), gfpre.reshape(1, -1),
                       gfpost.reshape(1, -1), wup, wdn)


def _dsa_sample_scores_kernel(pt_ref, *refs, n_pages):
    pages = refs[:n_pages]
    qi_ref, wi_ref, kinew_ref, o_ref = refs[n_pages:]
    qi = qi_ref[...]
    wi = wi_ref[...]

    def weighted(d):
        r = jnp.sum(wi * jnp.maximum(d, 0.0), axis=0, keepdims=True)
        return jnp.where(r == 0.0, 0.0, r)

    for p in range(n_pages):
        d = jnp.dot(qi, pages[p][...].astype(BF16), preferred_element_type=F32)
        o_ref[:, p * LANES:(p + 1) * LANES] = weighted(d)
    knew = kinew_ref[...].astype(BF16).astype(F32)
    d_new = jnp.sum(qi.astype(F32) * knew, axis=1, keepdims=True)
    lane = lax.broadcasted_iota(I32, (1, LANES), 1)
    o_ref[:, n_pages * LANES:(n_pages + 1) * LANES] = jnp.where(lane == 0, weighted(d_new), -jnp.inf)


def _dsa_sample_scores_call(page_table, pool_ki_t, layer, qi16, wi16, ki_new):
    nb, n_pages = page_table.shape
    ncol = (n_pages + 1) * LANES
    per = lambda b, pt: (b, 0, 0)
    page = lambda p: pl.BlockSpec((None, None, D_IDX, PAGE_SIZE),
                                  lambda b, pt: (layer, pt[b, p], 0, 0))
    return pl.pallas_call(
        functools.partial(_dsa_sample_scores_kernel, n_pages=n_pages),
        grid_spec=pltpu.PrefetchScalarGridSpec(
            num_scalar_prefetch=1, grid=(nb,),
            in_specs=[page(p) for p in range(n_pages)] + [
                pl.BlockSpec((None, 16, D_IDX), per),
                pl.BlockSpec((None, 16, 1), per),
                pl.BlockSpec((None, 1, D_IDX), per)],
            out_specs=pl.BlockSpec((None, 1, ncol), per)),
        out_shape=jax.ShapeDtypeStruct((nb, 1, ncol), F32),
        compiler_params=_cparams(("arbitrary",)),
        name="dsa_sample_scores")(page_table, *([pool_ki_t] * n_pages), qi16, wi16, ki_new)


def _dsa_sample_select_kernel(sc_ref, sel_ref, key_scr, hi_scr, lo_scr, *, topk, n_valid):
    ncol = sc_ref.shape[1]
    for p in range(ncol // LANES):
        rows = slice(p * LANES, (p + 1) * LANES)
        key = _sort_key(sc_ref[:, rows].T)
        key_scr[0, rows, :] = key
        hi_scr[0, rows, :] = (key >> 16).astype(I16)
        lo_scr[0, rows, :] = ((key & 0xFFFF) - 32768).astype(I16)
    thr, cut = _topk_threshold(key_scr, hi_scr, lo_scr, 1, ncol, topk)
    krow = lax.broadcasted_iota(I32, (LANES, LANES), 0)
    for p in range(ncol // LANES):
        rows = slice(p * LANES, (p + 1) * LANES)
        k = key_scr[0, rows, :]
        kpos = krow + p * LANES
        sel = ((k > thr) | ((k == thr) & (kpos < cut))) & (kpos < n_valid)
        sel_ref[:, rows] = jnp.where(sel, 1.0, 0.0).T


def _dsa_sample_select_call(sc, topk, n_valid):
    nb, ncol = sc.shape
    return pl.pallas_call(
        functools.partial(_dsa_sample_select_kernel, topk=topk, n_valid=n_valid),
        out_shape=jax.ShapeDtypeStruct((nb, ncol), F32),
        scratch_shapes=[pltpu.VMEM((1, ncol, nb), I32), pltpu.VMEM((1, ncol, nb), I16),
                        pltpu.VMEM((1, ncol, nb), I16)],
        compiler_params=pltpu.CompilerParams(vmem_limit_bytes=VMEM_LIMIT),
        name="dsa_sample_select")(sc)


def _dsa_sample_attn_kernel(pt_ref, *refs, n_pages):
    kpages = refs[:n_pages]
    vpages = refs[n_pages:2 * n_pages]
    q_ref, knew_ref, vnew_ref, sel_ref, o_ref = refs[2 * n_pages:]
    rows = lax.broadcasted_iota(I32, (16, W_A), 0)
    lanes = lax.broadcasted_iota(I32, (16, W_A), 1)
    own_head = lax.shift_right_logical(lanes, 6) == rows
    q_bd = jnp.where(own_head, jnp.broadcast_to(q_ref[...], (16, W_A)), 0.0).astype(BF16)
    first_pos = lax.broadcasted_iota(I32, (PAGE_SIZE, W_A), 0) == 0

    def new_tile(ref):
        return jnp.where(first_pos, jnp.broadcast_to(ref[...], (PAGE_SIZE, W_A)), 0.0).astype(BF16)

    def page_t(ref):
        return ref[...].reshape(W_A, PAGE_SIZE).astype(BF16)

    s_all = []
    for p in range(n_pages + 1):
        if p < n_pages:
            s = jnp.dot(q_bd, page_t(kpages[p]), preferred_element_type=F32)
        else:
            s = lax.dot_general(q_bd, new_tile(knew_ref), NT, preferred_element_type=F32)
        s_all.append(jnp.where(sel_ref[:, p * LANES:(p + 1) * LANES] > 0.5, s, NEG))
    m = s_all[0]
    for s in s_all[1:]:
        m = jnp.maximum(m, s)
    m = jnp.max(m, axis=1, keepdims=True)
    p_all = [jnp.exp2(s - m) for s in s_all]
    l = p_all[0]
    for pp in p_all[1:]:
        l = l + pp
    inv = 1.0 / jnp.sum(l, axis=1, keepdims=True)
    acc = jnp.zeros((16, W_A), F32)
    for p in range(n_pages + 1):
        pn = (p_all[p] * inv).astype(BF16)
        if p < n_pages:
            acc = acc + lax.dot_general(pn, page_t(vpages[p]), NT, preferred_element_type=F32)
        else:
            acc = acc + jnp.dot(pn, new_tile(vnew_ref), preferred_element_type=F32)
    o_ref[...] = jnp.sum(jnp.where(own_head, acc, 0.0), axis=0, keepdims=True)


def _dsa_sample_attn_call(page_table, pool_k_t, pool_v_t, layer, q, k_new, v_new, sel):
    nb, n_pages = page_table.shape
    per = lambda b, pt: (b, 0, 0)
    page = lambda p: pl.BlockSpec((None, None, H_A, HEAD_DIM, PAGE_SIZE),
                                  lambda b, pt: (layer, pt[b, p], 0, 0, 0))
    row_blk = pl.BlockSpec((None, 1, W_A), per)
    return pl.pallas_call(
        functools.partial(_dsa_sample_attn_kernel, n_pages=n_pages),
        grid_spec=pltpu.PrefetchScalarGridSpec(
            num_scalar_prefetch=1, grid=(nb,),
            in_specs=[page(p) for p in range(n_pages)] * 2 + [
                row_blk, row_blk, row_blk,
                pl.BlockSpec((None, 1, sel.shape[2]), per)],
            out_specs=row_blk),
        out_shape=jax.ShapeDtypeStruct((nb, 1, W_A), F32),
        compiler_params=_cparams(("arbitrary",)),
        name="dsa_sample_attn")(page_table, *([pool_k_t] * n_pages), *([pool_v_t] * n_pages),
                                q, k_new, v_new, sel)


def _ret_sample_kernel(q_ref, k_ref, v_ref, g_ref, s_ref, gret_ref, gam_ref, o_ref, sout_ref):
    v = v_ref[...]
    gam = gam_ref[...]

    def body(d, o):
        s_new = s_ref[d] * gam + k_ref[pl.ds(d, 1), :] * v
        sout_ref[d] = s_new
        return o + q_ref[pl.ds(d, 1), :] * s_new

    o = lax.fori_loop(0, HEAD_DIM, body, jnp.zeros(v.shape, F32))
    mu = jnp.mean(o, axis=0, keepdims=True)
    var = jnp.mean(jnp.square(o - mu), axis=0, keepdims=True)
    on = (o - mu) * lax.rsqrt(var + EPS)
    o_ref[...] = on * gret_ref[...] * _swish(g_ref[...])


def _ret_sample_call(qb, kb, vb, gb, state_t, layer, g_ret):
    nb = qb.shape[0]
    gam = jnp.exp(1.0 * _log_gamma())
    gam_b = jnp.broadcast_to(jnp.repeat(gam, HEAD_DIM)[:, None], (W_B, nb))
    gret_b = jnp.broadcast_to(g_ret[:, None], (W_B, nb))
    head = pl.BlockSpec((HEAD_DIM, nb), lambda h: (h, 0))
    o_t, s_new = pl.pallas_call(
        _ret_sample_kernel, grid=(H_B,),
        in_specs=[head, head, head, head,
                  pl.BlockSpec((None, None, HEAD_DIM, HEAD_DIM, nb), lambda h: (layer, h, 0, 0, 0)),
                  head, head],
        out_specs=(head, pl.BlockSpec((None, HEAD_DIM, HEAD_DIM, nb), lambda h: (h, 0, 0, 0))),
        out_shape=(jax.ShapeDtypeStruct((W_B, nb), F32),
                   jax.ShapeDtypeStruct((H_B, HEAD_DIM, HEAD_DIM, nb), F32)),
        compiler_params=_cparams(("arbitrary",)),
        name="ret_sample")(qb.T, kb.T, vb.T, gb.T, state_t, gret_b, gam_b)
    return o_t.T.astype(BF16), s_new


def _pack_w_in_t(w_t):
    d = w_t.shape[1]
    splits = np.cumsum(COL_SIZES)[:-1].tolist()
    qa, ka, va, qi, ki, wi, qb, kb, vb, gb = jnp.split(w_t, splits, axis=0)
    qi_pad = jnp.pad(qi.reshape(H_IDX, D_IDX, d), ((0, 0), (0, LANES - D_IDX), (0, 0))).reshape(-1, d)
    kiwi = jnp.concatenate([ki, wi, jnp.zeros((LANES - D_IDX - H_IDX, d), w_t.dtype)], axis=0)
    return jnp.concatenate([qa, ka, va, qi_pad, kiwi, qb, kb, vb, gb], axis=0).astype(BF16)


def _rope_tables(pos):
    half = HEAD_DIM // 2
    inv = ROPE_THETA ** (-jnp.arange(half, dtype=F32) / half)
    ang = pos[:, None] * inv[None, :]
    c = jnp.cos(ang)
    s = jnp.sin(ang)
    return jnp.concatenate([c, c, c, c], axis=1), jnp.concatenate([-s, s, -s, s], axis=1)


def kernel(x_prompt, x_sample, cache_k, cache_v, cache_kidx, state_ret, page_table, w_in, w_out,
           g_ret, g_mix_pre, g_mix_post, g_ffn_pre, g_ffn_post, w_up, w_down):
    batch, seq, _ = x_prompt.shape
    nb, dec_seq, _ = x_sample.shape
    depth = w_in.shape[0]
    n_pages = page_table.shape[1]
    past = n_pages * PAGE_SIZE
    assert dec_seq == 1 and seq % DSA_CH == 0 and nb == LANES
    tm_p = DSA_CH

    cos_p, sin_p = _rope_tables(jnp.arange(seq, dtype=F32))
    cos_s, sin_s = _rope_tables(past + jnp.arange(dec_seq, dtype=F32))
    cos_s = jnp.broadcast_to(cos_s, (nb, LANES))
    sin_s = jnp.broadcast_to(sin_s, (nb, LANES))
    pool_k_t = jnp.transpose(cache_k, (0, 1, 3, 4, 2))
    pool_v_t = jnp.transpose(cache_v, (0, 1, 3, 4, 2))
    pool_ki_t = jnp.swapaxes(cache_kidx, 2, 3)
    state_t = jnp.transpose(state_ret, (0, 2, 3, 4, 1))
    w_in_t = jnp.transpose(w_in, (2, 0, 1))
    wo_bf = w_out.astype(BF16)
    wup_bf = w_up.astype(BF16)
    wdn_bf = w_down.astype(BF16)
    topk_s = min(TOPK_MAX, (past + dec_seq) // 4)

    yp = x_prompt.reshape(batch * seq, D_MODEL)
    ys = x_sample.reshape(nb, D_MODEL)
    kp, vp, kip, sp, kss, vss, kis, ss = [], [], [], [], [], [], [], []
    for l in range(depth):
        w_pack = _pack_w_in_t(w_in_t[:, l, :])
        wo, wup, wdn = wo_bf[l], wup_bf[l], wdn_bf[l]
        gpre = g_mix_pre[l].reshape(1, D_MODEL)

        (qa, ka, ka_bf, va, va_t, qi, kiwi, kiwi_bf, ki, qb, kb, vb, gb) = _proj_call(
            yp, gpre, w_pack, cos_p, sin_p, tm_p)
        a = _dsa_prompt_call(qa, qi, kiwi, ka_bf, va_t, kiwi_bf, batch, seq)
        bmix, s_new = _ret_prompt_call(qb, kb, vb, gb, g_ret[l], batch, seq)
        yp = _outffn_call(a, bmix, yp, wo, g_mix_post[l], g_ffn_pre[l], g_ffn_post[l], wup, wdn, 256)
        kp.append(ka); vp.append(va); kip.append(ki); sp.append(s_new)

        (qa, ka, ka_bf, va, va_t, qi, kiwi, kiwi_bf, ki, qb, kb, vb, gb) = _proj_call(
            ys, gpre, w_pack, cos_s, sin_s, nb)
        qi16 = jnp.pad(qi.reshape(nb, H_IDX, LANES)[:, :, :D_IDX], ((0, 0), (0, 16 - H_IDX), (0, 0)))
        wi16 = jnp.pad(kiwi[:, D_IDX:D_IDX + H_IDX], ((0, 0), (0, 16 - H_IDX)))[:, :, None]
        sc = _dsa_sample_scores_call(page_table, pool_ki_t, l, qi16, wi16, ki.reshape(nb, 1, D_IDX))
        sel = _dsa_sample_select_call(sc.reshape(nb, -1), topk_s, past + dec_seq)
        a = _dsa_sample_attn_call(page_table, pool_k_t, pool_v_t, l,
                                  qa.astype(F32).reshape(nb, 1, W_A),
                                  ka.reshape(nb, 1, W_A), va.reshape(nb, 1, W_A),
                                  sel.reshape(nb, 1, -1))
        bmix, s_new = _ret_sample_call(qb, kb, vb, gb, state_t, l, g_ret[l])
        ys = _outffn_call(a.reshape(nb, W_A).astype(BF16), bmix, ys, wo, g_mix_post[l], g_ffn_pre[l],
                          g_ffn_post[l], wup, wdn, nb)
        kss.append(ka); vss.append(va); kis.append(ki); ss.append(s_new)

    def heads(xs, lead):
        return jnp.stack(xs).reshape((depth,) + lead + (H_A, HEAD_DIM))

    ret_s = jnp.transpose(jnp.stack(ss), (0, 4, 1, 2, 3))
    return (yp.reshape(batch, seq, D_MODEL), ys.reshape(nb, dec_seq, D_MODEL),
            heads(kp, (batch, seq)), heads(vp, (batch, seq)),
            jnp.stack(kip).reshape(depth, batch, seq, D_IDX), jnp.stack(sp),
            heads(kss, (nb, dec_seq)), heads(vss, (nb, dec_seq)),
            jnp.stack(kis).reshape(depth, nb, dec_seq, D_IDX), ret_s)
```

```python
import functools

import numpy as np
import jax
import jax.numpy as jnp
from jax import lax
from jax.experimental import pallas as pl
from jax.experimental.pallas import tpu as pltpu

F32 = jnp.float32
BF16 = jnp.bfloat16
I32 = jnp.int32

D_MODEL = 1024
HEAD_DIM = 64
H_A = 8
H_B = 8
H_IDX = 4
D_IDX = 64
W_A = H_A * HEAD_DIM
W_B = H_B * HEAD_DIM
D_FF = 4 * D_MODEL
TOPK_MAX = 256
PAGE_SIZE = 128
ROPE_THETA = 10000.0
EPS = 1e-6
Q_BLOCK = 128
RET_CHUNK = 128
COL_SIZES = (W_A, W_A, W_A, H_IDX * D_IDX, D_IDX, H_IDX, W_B, W_B, W_B, W_B)

LANES = 128
NEG = -1e30
INT_MIN = -2147483648
VMEM_LIMIT = 48 * 1024 * 1024

LOG2E = 1.4426950408889634
VT_ROWS = 144
VT_ALL = (H_A // 2) * VT_ROWS

NT = (((1,), (1,)), ((), ()))
TN = (((0,), (0,)), ((), ()))

C_QA, C_KA, C_VA, C_QI, C_KIWI, C_QB, C_KB, C_VB, C_GB = (
    0, 512, 1024, 1536, 2048, 2176, 2688, 3200, 3712)
N_PACK = 4224


def _cparams(sem):
    return pltpu.CompilerParams(dimension_semantics=sem, vmem_limit_bytes=VMEM_LIMIT)


def _rms(x, g):
    return x * lax.rsqrt(jnp.mean(x * x, axis=-1, keepdims=True) + EPS) * g


def _sort_key(x):
    bits = pltpu.bitcast(x, I32)
    return bits ^ ((bits >> 31) & 0x7FFFFFFF)


def _proj_kernel(x_ref, g_ref, w_ref, cos_ref, sin_ref,
                 qa_ref, ka_ref, kabf_ref, va_ref, vat_ref, qi_ref, kiwi_ref, kiwibf_ref,
                 ki_ref, qb_ref, kb_ref, vb_ref, gb_ref):
    x = x_ref[...]
    h = _rms(x, g_ref[...]).astype(BF16)
    cos = cos_ref[...]
    sin = sin_ref[...]
    lane = lax.broadcasted_iota(I32, cos.shape, 1)
    first_half = (lane % 64) < 32

    def rope(z, c, s):
        sw = jnp.where(first_half, pltpu.roll(z, 96, 1), pltpu.roll(z, 32, 1))
        return z * c + sw * s

    def proj(c0, n):
        return lax.dot_general(h, w_ref[c0:c0 + n, :], NT, preferred_element_type=F32)

    def roped(c0, scale):
        z = proj(c0, 512)
        parts = []
        for j in range(4):
            r = rope(z[:, j * LANES:(j + 1) * LANES], cos, sin)
            parts.append(r if scale is None else r * scale)
        return parts

    qa = roped(C_QA, HEAD_DIM ** -0.5 * LOG2E)
    for j in range(4):
        qa_ref[:, j * LANES:(j + 1) * LANES] = qa[j].astype(BF16)
    ka = roped(C_KA, None)
    for j in range(4):
        ka_ref[:, j * LANES:(j + 1) * LANES] = ka[j]
        kabf_ref[:, j * LANES:(j + 1) * LANES] = ka[j].astype(BF16)
    va = proj(C_VA, 512)
    va_ref[...] = va
    va_t = va.T
    ones = jnp.ones((VT_ROWS - LANES, va_t.shape[1]), BF16)
    for p in range(H_A // 2):
        vat_ref[p * VT_ROWS:p * VT_ROWS + LANES, :] = va_t[p * LANES:(p + 1) * LANES, :].astype(BF16)
        vat_ref[p * VT_ROWS + LANES:(p + 1) * VT_ROWS, :] = ones
    qi = roped(C_QI, None)
    for j in range(4):
        qi_ref[:, j * LANES:(j + 1) * LANES] = qi[j].astype(BF16)
    zk = proj(C_KIWI, LANES)
    is_key = lane < 64
    kiwi = rope(zk, jnp.where(is_key, cos, (H_IDX ** -0.5) * (D_IDX ** -0.5)),
                jnp.where(is_key, sin, 0.0))
    kiwi_ref[...] = kiwi
    kiwibf_ref[...] = kiwi.astype(BF16)
    ki_ref[...] = kiwi[:, 0:64]
    qb = roped(C_QB, None)
    for j in range(4):
        qb_ref[:, j * LANES:(j + 1) * LANES] = qb[j]
    kb = roped(C_KB, HEAD_DIM ** -0.5)
    for j in range(4):
        kb_ref[:, j * LANES:(j + 1) * LANES] = kb[j]
    vb_ref[...] = proj(C_VB, 512)
    gb_ref[...] = proj(C_GB, 512)


def _proj_call(x, g, w_pack_t, cos_t, sin_t, tm):
    m = x.shape[0]
    n_t = cos_t.shape[0] // tm
    row = lambda i: (i, 0)
    const = lambda i: (0, 0)
    tab = lambda i: (i % n_t, 0)
    wide = lambda dt: jax.ShapeDtypeStruct((m, 512), dt)
    out_shape = (wide(BF16), wide(F32), wide(BF16), wide(F32),
                 jax.ShapeDtypeStruct((m // tm, VT_ALL, tm), BF16), wide(BF16),
                 jax.ShapeDtypeStruct((m, LANES), F32), jax.ShapeDtypeStruct((m, LANES), BF16),
                 jax.ShapeDtypeStruct((m, 64), F32), wide(F32), wide(F32), wide(F32), wide(F32))
    out_specs = tuple(
        pl.BlockSpec((None, VT_ALL, tm), lambda i: (i, 0, 0)) if len(s.shape) == 3
        else pl.BlockSpec((tm, s.shape[1]), row) for s in out_shape)
    return pl.pallas_call(
        _proj_kernel, grid=(m // tm,),
        in_specs=[pl.BlockSpec((tm, D_MODEL), row),
                  pl.BlockSpec((1, D_MODEL), const),
                  pl.BlockSpec((N_PACK, D_MODEL), const, pipeline_mode=pl.Buffered(1)),
                  pl.BlockSpec((tm, LANES), tab),
                  pl.BlockSpec((tm, LANES), tab)],
        out_specs=out_specs, out_shape=out_shape,
        compiler_params=_cparams(("arbitrary",)), name="proj")(x, g, w_pack_t, cos_t, sin_t)


I16 = jnp.int16
I16_MIN = -32768
I16_MAX = 32767
COUNT_ROWS = 32


def _store_split_keys(key, idx, key_scr, hi_scr, lo_scr):
    key_scr[idx] = key
    hi_scr[idx] = (key >> 16).astype(I16)
    lo_scr[idx] = ((key & 0xFFFF) - 32768).astype(I16)


def _count16(scr, nc, ch, pred):
    one = jnp.ones((), I16)
    zero = jnp.zeros((), I16)

    def body(c, acc):
        m = jnp.where(pred(scr[c]), one, zero)
        for r in range(ch // COUNT_ROWS):
            acc = acc + m[r * COUNT_ROWS:(r + 1) * COUNT_ROWS]
        return acc

    acc = lax.fori_loop(0, nc, body, jnp.zeros((COUNT_ROWS, LANES), I16))
    return jnp.sum(acc.astype(F32), axis=0, keepdims=True)


def _search16(scr, nc, ch, target, total):
    cnt0 = _count16(scr, nc, ch, lambda k: k >= jnp.zeros((), I16))
    ok0 = cnt0 >= target
    init = (jnp.where(ok0, 0, I16_MIN), jnp.where(ok0, cnt0, total), jnp.where(ok0, 0.0, cnt0))

    def bit_body(j, carry):
        lo, lo_cnt, up_cnt = carry
        cand = lo + jnp.left_shift(jnp.int32(1), 14 - j)
        c16 = cand.astype(I16)
        cnt = _count16(scr, nc, ch, lambda k: k >= c16)
        ok = cnt >= target
        return jnp.where(ok, cand, lo), jnp.where(ok, cnt, lo_cnt), jnp.where(ok, up_cnt, cnt)

    return lax.fori_loop(0, 15, bit_body, init)


def _topk_threshold(key_scr, hi_scr, lo_scr, nc, ch, topk):
    assert ch >= topk and ch % COUNT_ROWS == 0
    kf = float(topk)
    total = (nc * ch).astype(F32) if hasattr(nc, "astype") else float(nc * ch)
    t_hi, _, above_hi = _search16(hi_scr, nc, ch, kf, total)
    t_hi16 = t_hi.astype(I16)
    need_lo = kf - above_hi

    def bucket_body(c, carry):
        lo_scr[c] = jnp.where(hi_scr[c] == t_hi16, lo_scr[c], jnp.full((), I16_MIN, I16))
        return carry

    lax.fori_loop(0, nc, bucket_body, 0)
    t_lo, at_lo, above_lo = _search16(lo_scr, nc, ch, need_lo, total)
    need_tie = need_lo - above_lo
    thr = (t_hi << 16) | (t_lo + 32768)

    n_tied = at_lo - above_lo
    krow = lax.broadcasted_iota(I32, (ch, LANES), 0)

    def tie_search():
        def tie_prep(c, carry):
            lo_scr[c] = jnp.where(key_scr[c] == thr, krow + c * ch, I16_MAX).astype(I16)
            return carry

        lax.fori_loop(0, nc, tie_prep, 0)

        def tie_body(j, cut):
            cand = cut + jnp.left_shift(jnp.int32(1), 12 - j)
            c16 = cand.astype(I16)
            f = _count16(lo_scr, nc, ch, lambda k: k < c16)
            return jnp.where(f <= need_tie, cand, cut)

        return lax.fori_loop(0, 13, tie_body, jnp.zeros((1, LANES), I32))

    cut = lax.cond(jnp.max(n_tied - need_tie) > 0.0, tie_search,
                   lambda: jnp.full((1, LANES), I16_MAX, I32))
    return thr, cut


DSA_CH = 512


def _dsa_prompt_kernel(qa_ref, qi_ref, kiwi_ref, ka_ref, vat_ref, kibf_ref, o_ref,
                       key_scr, hi_scr, lo_scr, bias_scr, acc_scr, qcat_scr, qicat_scr,
                       sa_scr, sb_scr, *, topk):
    ch = DSA_CH
    n_pair = H_A // 2
    i = pl.program_id(1)
    t0 = i * Q_BLOCK
    nc = lax.div(t0 + Q_BLOCK + ch - 1, ch)
    qpos = t0 + lax.broadcasted_iota(I32, (ch, LANES), 1)
    krow = lax.broadcasted_iota(I32, (ch, LANES), 0)
    wi_t = kiwi_ref[...].T

    d_lo = lax.broadcasted_iota(I32, (LANES, Q_BLOCK), 0) < HEAD_DIM
    for p in range(n_pair):
        qt = qa_ref[:, p * LANES:(p + 1) * LANES].astype(F32).T
        qcat_scr[p, :, 0:Q_BLOCK] = jnp.where(d_lo, qt, 0.0).astype(BF16)
        qcat_scr[p, :, Q_BLOCK:2 * Q_BLOCK] = jnp.where(d_lo, 0.0, qt).astype(BF16)
    for g in range(H_IDX // 2):
        for e in range(2):
            h = 2 * g + e
            qicat_scr[g, :, e * Q_BLOCK:(e + 1) * Q_BLOCK] = (
                qi_ref[:, h * LANES:(h + 1) * LANES].astype(F32).T.astype(BF16))

    def score_body(c, carry):
        off = pl.multiple_of(c * ch, ch)
        kic = kibf_ref[pl.ds(off, ch), :]
        acc = jnp.zeros((ch, LANES), F32)
        for g in range(H_IDX // 2):
            d = jnp.dot(kic, qicat_scr[g], preferred_element_type=F32)
            acc = (acc + wi_t[64 + 2 * g:65 + 2 * g, :] * jnp.maximum(d[:, 0:LANES], 0.0)
                   + wi_t[65 + 2 * g:66 + 2 * g, :] * jnp.maximum(d[:, LANES:2 * LANES], 0.0))
        acc = jnp.where(acc == 0.0, 0.0, acc)
        sc = jnp.where(krow + c * ch <= qpos, acc, -jnp.inf)
        _store_split_keys(_sort_key(sc), sel_index(c), key_scr, hi_scr, lo_scr)
        return carry

    def sel_index(c):
        return (lax.shift_right_logical(c, 1), pl.ds(pl.multiple_of((c & 1) * ch, ch), ch))

    lax.fori_loop(0, nc, score_body, 0)

    odd = (nc & 1) == 1

    @pl.when(odd)
    def _():
        idx = sel_index(nc)
        key_scr[idx] = jnp.full((ch, LANES), INT_MIN, I32)
        hi_scr[idx] = jnp.full((ch, LANES), I16_MIN, I16)
        lo_scr[idx] = jnp.full((ch, LANES), I16_MIN, I16)
        bias_scr[nc] = jnp.full((ch, LANES), NEG, F32)

    n2 = lax.shift_right_logical(nc + 1, 1)
    thr, cut = _topk_threshold(key_scr, hi_scr, lo_scr, n2, 2 * ch, topk)

    def bias_body(c, carry):
        k = key_scr[sel_index(c)]
        kpos = krow + c * ch
        sel = ((k > thr) | ((k == thr) & (kpos < cut))) & (kpos <= qpos)
        bias_scr[c] = jnp.where(sel, 0.0, NEG)
        return carry

    lax.fori_loop(0, nc, bias_body, 0)

    row_lo = lax.broadcasted_iota(I32, (LANES, Q_BLOCK), 0) < HEAD_DIM
    acc_scr[...] = jnp.zeros_like(acc_scr)
    last_chunk = ka_ref.shape[0] // ch - 1

    def qk_scores(c, p):
        off = pl.multiple_of(jnp.minimum(c, last_chunk) * ch, ch)
        return jnp.dot(ka_ref[pl.ds(off, ch), p * LANES:(p + 1) * LANES], qcat_scr[p],
                       preferred_element_type=F32)

    for p in range(n_pair):
        sa_scr[p] = qk_scores(0, p)

    def half_step(c, cur_scr, nxt_scr, carry):
        b = bias_scr[c]
        new = []
        for p in range(n_pair):
            nxt_scr[p] = qk_scores(c + 1, p)
        for p in range(n_pair):
            s = cur_scr[p]
            alphas, pts, ms = [], [], []
            for e in range(2):
                m = carry[4 * p + 2 * e]
                sb = s[:, e * LANES:(e + 1) * LANES] + b
                m_new = jnp.maximum(m, jnp.max(sb, axis=0, keepdims=True))
                alphas.append(jnp.exp2(m - m_new))
                pts.append(jnp.exp2(sb - m_new).astype(BF16))
                ms.append(m_new)
            pv = jnp.dot(vat_ref[c, p * VT_ROWS:(p + 1) * VT_ROWS, :],
                         jnp.concatenate(pts, axis=1), preferred_element_type=F32)
            for e in range(2):
                l = carry[4 * p + 2 * e + 1]
                new += [ms[e], alphas[e] * l + pv[LANES:LANES + 1, e * LANES:(e + 1) * LANES]]
            acc_scr[p] = (acc_scr[p] * jnp.where(row_lo, alphas[0], alphas[1])
                          + jnp.where(row_lo, pv[0:LANES, 0:LANES], pv[0:LANES, LANES:2 * LANES]))
        return tuple(new)

    def att_body(j, carry):
        carry = half_step(2 * j, sa_scr, sb_scr, carry)
        return half_step(2 * j + 1, sb_scr, sa_scr, carry)

    init = (jnp.full((1, LANES), NEG, F32), jnp.zeros((1, LANES), F32)) * H_A
    stats = lax.fori_loop(0, n2, att_body, init)
    for p in range(n_pair):
        l_pair = jnp.where(row_lo, stats[4 * p + 1], stats[4 * p + 3])
        o_ref[:, p * LANES:(p + 1) * LANES] = (acc_scr[p] / l_pair).T.astype(BF16)


def _dsa_prompt_call(qa, qi, kiwi, ka_bf, va_t, kiwi_bf, batch, seq):
    nq = seq // Q_BLOCK
    topk = min(TOPK_MAX, seq // 4)
    nch = seq // DSA_CH
    blk = lambda b, i: (b * nq + i, 0)
    full = lambda b, i: (b, 0)
    return pl.pallas_call(
        functools.partial(_dsa_prompt_kernel, topk=topk), grid=(batch, nq),
        in_specs=[pl.BlockSpec((Q_BLOCK, W_A), blk),
                  pl.BlockSpec((Q_BLOCK, 512), blk),
                  pl.BlockSpec((Q_BLOCK, LANES), blk),
                  pl.BlockSpec((seq, W_A), full),
                  pl.BlockSpec((nch, VT_ALL, DSA_CH), lambda b, i: (b, 0, 0)),
                  pl.BlockSpec((seq, LANES), full)],
        out_specs=pl.BlockSpec((Q_BLOCK, W_A), blk),
        out_shape=jax.ShapeDtypeStruct((batch * seq, W_A), BF16),
        scratch_shapes=[pltpu.VMEM((nch // 2, 2 * DSA_CH, LANES), I32),
                        pltpu.VMEM((nch // 2, 2 * DSA_CH, LANES), I16),
                        pltpu.VMEM((nch // 2, 2 * DSA_CH, LANES), I16),
                        pltpu.VMEM((nch, DSA_CH, LANES), F32),
                        pltpu.VMEM((H_A // 2, LANES, Q_BLOCK), F32),
                        pltpu.VMEM((H_A // 2, LANES, 2 * Q_BLOCK), BF16),
                        pltpu.VMEM((H_IDX // 2, LANES, 2 * Q_BLOCK), BF16),
                        pltpu.VMEM((H_A // 2, DSA_CH, 2 * Q_BLOCK), F32),
                        pltpu.VMEM((H_A // 2, DSA_CH, 2 * Q_BLOCK), F32)],
        compiler_params=_cparams(("arbitrary", "arbitrary")),
        name="dsa_prompt")(qa, qi, kiwi, ka_bf, va_t, kiwi_bf)


def _swish(g):
    return g * (1.0 / (1.0 + jnp.exp(-g)))


def _ret_prompt_kernel(q_ref, k_ref, v_ref, g_ref, dmask_ref, qdec_ref, kdec_ref, cdec_ref,
                       gret_ref, avg_ref, o_ref, sout_ref, s_scr):
    c = pl.program_id(1)

    @pl.when(c == 0)
    def _():
        s_scr[...] = jnp.zeros_like(s_scr)

    lane_lo = lax.broadcasted_iota(I32, (RET_CHUNK, LANES), 1) < HEAD_DIM
    avg = avg_ref[...]
    same_head = (lax.broadcasted_iota(I32, (LANES, LANES), 0) < HEAD_DIM) == (
        lax.broadcasted_iota(I32, (LANES, LANES), 1) < HEAD_DIM)
    pairs = range(H_B // 2)
    cols = [slice(p * LANES, (p + 1) * LANES) for p in pairs]

    def head_mean(x):
        x_hi = x.astype(BF16)
        x_lo = (x - x_hi.astype(F32)).astype(BF16)
        return (jnp.dot(x_hi, avg, preferred_element_type=F32)
                + jnp.dot(x_lo, avg, preferred_element_type=F32))

    qb = [q_ref[:, cols[p]].astype(BF16) for p in pairs]
    kb = [k_ref[:, cols[p]].astype(BF16) for p in pairs]
    vb = [v_ref[:, cols[p]].astype(BF16) for p in pairs]
    att = []
    for p in pairs:
        zero = jnp.zeros_like(qb[p])
        for e, qe in enumerate((jnp.where(lane_lo, qb[p], zero), jnp.where(lane_lo, zero, qb[p]))):
            att.append(lax.dot_general(qe, kb[p], NT, preferred_element_type=F32) * dmask_ref[2 * p + e])
    o = []
    for p in pairs:
        intra = [jnp.dot(att[2 * p + e].astype(BF16), vb[p], preferred_element_type=F32)
                 for e in range(2)]
        s_bd = s_scr[p]
        o.append(jnp.where(lane_lo, intra[0], intra[1])
                 + jnp.dot(qb[p], s_bd.astype(BF16), preferred_element_type=F32) * qdec_ref[:, cols[p]])
        kd = (k_ref[:, cols[p]] * kdec_ref[:, cols[p]]).astype(BF16)
        upd = lax.dot_general(kd, vb[p], TN, preferred_element_type=F32)
        s_scr[p] = s_bd * cdec_ref[p] + jnp.where(same_head, upd, 0.0)
    dev = [o[p] - head_mean(o[p]) for p in pairs]
    var = [head_mean(dev[p] * dev[p]) for p in pairs]
    for p in pairs:
        on = dev[p] * lax.rsqrt(var[p] + EPS)
        o_ref[:, cols[p]] = (on * gret_ref[:, cols[p]] * _swish(g_ref[:, cols[p]])).astype(BF16)

    @pl.when(c == pl.num_programs(1) - 1)
    def _():
        for p in range(H_B // 2):
            s_bd = s_scr[p]
            sout_ref[2 * p] = s_bd[0:HEAD_DIM, 0:HEAD_DIM]
            sout_ref[2 * p + 1] = s_bd[HEAD_DIM:LANES, HEAD_DIM:LANES]


def _log_gamma():
    return jnp.log1p(-jnp.exp2(-5.0 - jnp.arange(H_B, dtype=F32)))


def _ret_tables(chunk):
    lg = _log_gamma()
    i = jnp.arange(chunk, dtype=F32)
    diff = i[:, None] - i[None, :]
    dmask = jnp.where(diff[None] >= 0, jnp.exp(jnp.maximum(diff, 0.0)[None] * lg[:, None, None]), 0.0)
    q_dec = jnp.exp((i[:, None] + 1.0) * lg[None, :])
    k_dec = jnp.exp((chunk - 1.0 - i)[:, None] * lg[None, :])
    c_dec = jnp.exp(chunk * lg)
    return dmask, q_dec, k_dec, c_dec


def _ret_prompt_call(qb, kb, vb, gb, g_ret, batch, seq):
    chunk = RET_CHUNK
    n = seq // chunk
    dmask, q_dec, k_dec, c_dec = _ret_tables(chunk)
    qdec = jnp.repeat(q_dec, HEAD_DIM, axis=1)
    kdec = jnp.repeat(k_dec, HEAD_DIM, axis=1)
    cdec = jnp.broadcast_to(jnp.repeat(c_dec, HEAD_DIM).reshape(H_B // 2, LANES, 1),
                            (H_B // 2, LANES, LANES))
    head_of = jnp.arange(LANES) // HEAD_DIM
    avg = jnp.where(head_of[:, None] == head_of[None, :], 1.0 / HEAD_DIM, 0.0).astype(BF16)
    blk = lambda b, c: (b * n + c, 0)
    c2 = lambda b, c: (0, 0)
    c3 = lambda b, c: (0, 0, 0)
    return pl.pallas_call(
        _ret_prompt_kernel, grid=(batch, n),
        in_specs=[pl.BlockSpec((chunk, W_B), blk)] * 4 + [
            pl.BlockSpec((H_B, chunk, chunk), c3),
            pl.BlockSpec((chunk, W_B), c2),
            pl.BlockSpec((chunk, W_B), c2),
            pl.BlockSpec((H_B // 2, LANES, LANES), c3),
            pl.BlockSpec((1, W_B), c2),
            pl.BlockSpec((LANES, LANES), c2)],
        out_specs=(pl.BlockSpec((chunk, W_B), blk),
                   pl.BlockSpec((None, H_B, HEAD_DIM, HEAD_DIM), lambda b, c: (b, 0, 0, 0))),
        out_shape=(jax.ShapeDtypeStruct((batch * seq, W_B), BF16),
                   jax.ShapeDtypeStruct((batch, H_B, HEAD_DIM, HEAD_DIM), F32)),
        scratch_shapes=[pltpu.VMEM((H_B // 2, LANES, LANES), F32)],
        compiler_params=_cparams(("arbitrary", "arbitrary")),
        name="ret_prompt")(qb, kb, vb, gb, dmask, qdec, kdec, cdec, g_ret.reshape(1, W_B), avg)


FF_CHUNK = 1024


def _outffn_kernel(a_ref, b_ref, x_ref, wo_ref, gpost_ref, gfpre_ref, gfpost_ref,
                   wup_ref, wdn_ref, y_ref):
    m = (jnp.dot(a_ref[...], wo_ref[0:W_A, :], preferred_element_type=F32)
         + jnp.dot(b_ref[...], wo_ref[W_A:D_MODEL, :], preferred_element_type=F32))
    x1 = x_ref[...] + _rms(m, gpost_ref[...])
    hn = _rms(x1, gfpre_ref[...]).astype(BF16)
    f = jnp.zeros(x1.shape, F32)
    for c in range(D_FF // FF_CHUNK):
        cs = slice(c * FF_CHUNK, (c + 1) * FF_CHUNK)
        u = jnp.dot(hn, wup_ref[:, cs], preferred_element_type=F32)
        u = jnp.square(jnp.maximum(u, 0.0)).astype(BF16)
        f = f + jnp.dot(u, wdn_ref[cs, :], preferred_element_type=F32)
    y_ref[...] = x1 + _rms(f, gfpost_ref[...])


def _outffn_call(a, b, x, wo, gpost, gfpre, gfpost, wup, wdn, tm):
    m = x.shape[0]
    row = lambda i: (i, 0)
    const = lambda i: (0, 0)
    once = dict(pipeline_mode=pl.Buffered(1))
    return pl.pallas_call(
        _outffn_kernel, grid=(m // tm,),
        in_specs=[pl.BlockSpec((tm, W_A), row), pl.BlockSpec((tm, W_B), row),
                  pl.BlockSpec((tm, D_MODEL), row),
                  pl.BlockSpec((D_MODEL, D_MODEL), const, **once),
                  pl.BlockSpec((1, D_MODEL), const), pl.BlockSpec((1, D_MODEL), const),
                  pl.BlockSpec((1, D_MODEL), const),
                  pl.BlockSpec((D_MODEL, D_FF), const, **once),
                  pl.BlockSpec((D_FF, D_MODEL), const, **once)],
        out_specs=pl.BlockSpec((tm, D_MODEL), row),
        out_shape=jax.ShapeDtypeStruct((m, D_MODEL), F32),
        compiler_params=_cparams(("arbitrary",)),
        name="outffn")(a, b, x, wo, gpost.reshape(1, -1), gfpre.reshape(1, -1),
                       gfpost.reshape(1, -1), wup, wdn)


def _dsa_sample_scores_kernel(pt_ref, *refs, n_pages):
    pages = refs[:n_pages]
    qi_ref, wi_ref, kinew_ref, o_ref = refs[n_pages:]
    qi = qi_ref[...]
    wi = wi_ref[...]

    def weighted(d):
        r = jnp.sum(wi * jnp.maximum(d, 0.0), axis=0, keepdims=True)
        return jnp.where(r == 0.0, 0.0, r)

    for p in range(n_pages):
        d = jnp.dot(qi, pages[p][...].astype(BF16), preferred_element_type=F32)
        o_ref[:, p * LANES:(p + 1) * LANES] = weighted(d)
    knew = kinew_ref[...].astype(BF16).astype(F32)
    d_new = jnp.sum(qi.astype(F32) * knew, axis=1, keepdims=True)
    lane = lax.broadcasted_iota(I32, (1, LANES), 1)
    o_ref[:, n_pages * LANES:(n_pages + 1) * LANES] = jnp.where(lane == 0, weighted(d_new), -jnp.inf)


def _dsa_sample_scores_call(page_table, pool_ki_t, layer, qi16, wi16, ki_new):
    nb, n_pages = page_table.shape
    ncol = (n_pages + 1) * LANES
    per = lambda b, pt: (b, 0, 0)
    page = lambda p: pl.BlockSpec((None, None, D_IDX, PAGE_SIZE),
                                  lambda b, pt: (layer, pt[b, p], 0, 0))
    return pl.pallas_call(
        functools.partial(_dsa_sample_scores_kernel, n_pages=n_pages),
        grid_spec=pltpu.PrefetchScalarGridSpec(
            num_scalar_prefetch=1, grid=(nb,),
            in_specs=[page(p) for p in range(n_pages)] + [
                pl.BlockSpec((None, 16, D_IDX), per),
                pl.BlockSpec((None, 16, 1), per),
                pl.BlockSpec((None, 1, D_IDX), per)],
            out_specs=pl.BlockSpec((None, 1, ncol), per)),
        out_shape=jax.ShapeDtypeStruct((nb, 1, ncol), F32),
        compiler_params=_cparams(("arbitrary",)),
        name="dsa_sample_scores")(page_table, *([pool_ki_t] * n_pages), qi16, wi16, ki_new)


def _dsa_sample_select_kernel(sc_ref, sel_ref, key_scr, hi_scr, lo_scr, *, topk, n_valid):
    ncol = sc_ref.shape[1]
    for p in range(ncol // LANES):
        rows = slice(p * LANES, (p + 1) * LANES)
        key = _sort_key(sc_ref[:, rows].T)
        key_scr[0, rows, :] = key
        hi_scr[0, rows, :] = (key >> 16).astype(I16)
        lo_scr[0, rows, :] = ((key & 0xFFFF) - 32768).astype(I16)
    thr, cut = _topk_threshold(key_scr, hi_scr, lo_scr, 1, ncol, topk)
    krow = lax.broadcasted_iota(I32, (LANES, LANES), 0)
    for p in range(ncol // LANES):
        rows = slice(p * LANES, (p + 1) * LANES)
        k = key_scr[0, rows, :]
        kpos = krow + p * LANES
        sel = ((k > thr) | ((k == thr) & (kpos < cut))) & (kpos < n_valid)
        sel_ref[:, rows] = jnp.where(sel, 1.0, 0.0).T


def _dsa_sample_select_call(sc, topk, n_valid):
    nb, ncol = sc.shape
    return pl.pallas_call(
        functools.partial(_dsa_sample_select_kernel, topk=topk, n_valid=n_valid),
        out_shape=jax.ShapeDtypeStruct((nb, ncol), F32),
        scratch_shapes=[pltpu.VMEM((1, ncol, nb), I32), pltpu.VMEM((1, ncol, nb), I16),
                        pltpu.VMEM((1, ncol, nb), I16)],
        compiler_params=pltpu.CompilerParams(vmem_limit_bytes=VMEM_LIMIT),
        name="dsa_sample_select")(sc)


def _dsa_sample_attn_kernel(pt_ref, *refs, n_pages):
    kpages = refs[:n_pages]
    vpages = refs[n_pages:2 * n_pages]
    q_ref, knew_ref, vnew_ref, sel_ref, o_ref = refs[2 * n_pages:]
    rows = lax.broadcasted_iota(I32, (16, W_A), 0)
    lanes = lax.broadcasted_iota(I32, (16, W_A), 1)
    own_head = lax.shift_right_logical(lanes, 6) == rows
    q_bd = jnp.where(own_head, jnp.broadcast_to(q_ref[...], (16, W_A)), 0.0).astype(BF16)
    first_pos = lax.broadcasted_iota(I32, (PAGE_SIZE, W_A), 0) == 0

    def new_tile(ref):
        return jnp.where(first_pos, jnp.broadcast_to(ref[...], (PAGE_SIZE, W_A)), 0.0).astype(BF16)

    def page_t(ref):
        return ref[...].reshape(W_A, PAGE_SIZE).astype(BF16)

    s_all = []
    for p in range(n_pages + 1):
        if p < n_pages:
            s = jnp.dot(q_bd, page_t(kpages[p]), preferred_element_type=F32)
        else:
            s = lax.dot_general(q_bd, new_tile(knew_ref), NT, preferred_element_type=F32)
        s_all.append(jnp.where(sel_ref[:, p * LANES:(p + 1) * LANES] > 0.5, s, NEG))
    m = s_all[0]
    for s in s_all[1:]:
        m = jnp.maximum(m, s)
    m = jnp.max(m, axis=1, keepdims=True)
    p_all = [jnp.exp2(s - m) for s in s_all]
    l = p_all[0]
    for pp in p_all[1:]:
        l = l + pp
    inv = 1.0 / jnp.sum(l, axis=1, keepdims=True)
    acc = jnp.zeros((16, W_A), F32)
    for p in range(n_pages + 1):
        pn = (p_all[p] * inv).astype(BF16)
        if p < n_pages:
            acc = acc + lax.dot_general(pn, page_t(vpages[p]), NT, preferred_element_type=F32)
        else:
            acc = acc + jnp.dot(pn, new_tile(vnew_ref), preferred_element_type=F32)
    o_ref[...] = jnp.sum(jnp.where(own_head, acc, 0.0), axis=0, keepdims=True)


def _dsa_sample_attn_call(page_table, pool_k_t, pool_v_t, layer, q, k_new, v_new, sel):
    nb, n_pages = page_table.shape
    per = lambda b, pt: (b, 0, 0)
    page = lambda p: pl.BlockSpec((None, None, H_A, HEAD_DIM, PAGE_SIZE),
                                  lambda b, pt: (layer, pt[b, p], 0, 0, 0))
    row_blk = pl.BlockSpec((None, 1, W_A), per)
    return pl.pallas_call(
        functools.partial(_dsa_sample_attn_kernel, n_pages=n_pages),
        grid_spec=pltpu.PrefetchScalarGridSpec(
            num_scalar_prefetch=1, grid=(nb,),
            in_specs=[page(p) for p in range(n_pages)] * 2 + [
                row_blk, row_blk, row_blk,
                pl.BlockSpec((None, 1, sel.shape[2]), per)],
            out_specs=row_blk),
        out_shape=jax.ShapeDtypeStruct((nb, 1, W_A), F32),
        compiler_params=_cparams(("arbitrary",)),
        name="dsa_sample_attn")(page_table, *([pool_k_t] * n_pages), *([pool_v_t] * n_pages),
                                q, k_new, v_new, sel)


def _ret_sample_kernel(q_ref, k_ref, v_ref, g_ref, s_ref, gret_ref, gam_ref, o_ref, sout_ref):
    v = v_ref[...]
    gam = gam_ref[...]

    def body(d, o):
        s_new = s_ref[d] * gam + k_ref[pl.ds(d, 1), :] * v
        sout_ref[d] = s_new
        return o + q_ref[pl.ds(d, 1), :] * s_new

    o = lax.fori_loop(0, HEAD_DIM, body, jnp.zeros(v.shape, F32))
    mu = jnp.mean(o, axis=0, keepdims=True)
    var = jnp.mean(jnp.square(o - mu), axis=0, keepdims=True)
    on = (o - mu) * lax.rsqrt(var + EPS)
    o_ref[...] = on * gret_ref[...] * _swish(g_ref[...])


def _ret_sample_call(qb, kb, vb, gb, state_t, layer, g_ret):
    nb = qb.shape[0]
    gam = jnp.exp(1.0 * _log_gamma())
    gam_b = jnp.broadcast_to(jnp.repeat(gam, HEAD_DIM)[:, None], (W_B, nb))
    gret_b = jnp.broadcast_to(g_ret[:, None], (W_B, nb))
    head = pl.BlockSpec((HEAD_DIM, nb), lambda h: (h, 0))
    o_t, s_new = pl.pallas_call(
        _ret_sample_kernel, grid=(H_B,),
        in_specs=[head, head, head, head,
                  pl.BlockSpec((None, None, HEAD_DIM, HEAD_DIM, nb), lambda h: (layer, h, 0, 0, 0)),
                  head, head],
        out_specs=(head, pl.BlockSpec((None, HEAD_DIM, HEAD_DIM, nb), lambda h: (h, 0, 0, 0))),
        out_shape=(jax.ShapeDtypeStruct((W_B, nb), F32),
                   jax.ShapeDtypeStruct((H_B, HEAD_DIM, HEAD_DIM, nb), F32)),
        compiler_params=_cparams(("arbitrary",)),
        name="ret_sample")(qb.T, kb.T, vb.T, gb.T, state_t, gret_b, gam_b)
    return o_t.T.astype(BF16), s_new


def _pack_w_in_t(w_t):
    d = w_t.shape[1]
    splits = np.cumsum(COL_SIZES)[:-1].tolist()
    qa, ka, va, qi, ki, wi, qb, kb, vb, gb = jnp.split(w_t, splits, axis=0)
    qi_pad = jnp.pad(qi.reshape(H_IDX, D_IDX, d), ((0, 0), (0, LANES - D_IDX), (0, 0))).reshape(-1, d)
    kiwi = jnp.concatenate([ki, wi, jnp.zeros((LANES - D_IDX - H_IDX, d), w_t.dtype)], axis=0)
    return jnp.concatenate([qa, ka, va, qi_pad, kiwi, qb, kb, vb, gb], axis=0).astype(BF16)


def _rope_tables(pos):
    half = HEAD_DIM // 2
    inv = ROPE_THETA ** (-jnp.arange(half, dtype=F32) / half)
    ang = pos[:, None] * inv[None, :]
    c = jnp.cos(ang)
    s = jnp.sin(ang)
    return jnp.concatenate([c, c, c, c], axis=1), jnp.concatenate([-s, s, -s, s], axis=1)


def kernel(x_prompt, x_sample, cache_k, cache_v, cache_kidx, state_ret, page_table, w_in, w_out,
           g_ret, g_mix_pre, g_mix_post, g_ffn_pre, g_ffn_post, w_up, w_down):
    batch, seq, _ = x_prompt.shape
    nb, dec_seq, _ = x_sample.shape
    depth = w_in.shape[0]
    n_pages = page_table.shape[1]
    past = n_pages * PAGE_SIZE
    assert dec_seq == 1 and seq % DSA_CH == 0 and nb == LANES
    tm_p = DSA_CH

    cos_p, sin_p = _rope_tables(jnp.arange(seq, dtype=F32))
    cos_s, sin_s = _rope_tables(past + jnp.arange(dec_seq, dtype=F32))
    cos_s = jnp.broadcast_to(cos_s, (nb, LANES))
    sin_s = jnp.broadcast_to(sin_s, (nb, LANES))
    pool_k_t = jnp.transpose(cache_k, (0, 1, 3, 4, 2))
    pool_v_t = jnp.transpose(cache_v, (0, 1, 3, 4, 2))
    pool_ki_t = jnp.swapaxes(cache_kidx, 2, 3)
    state_t = jnp.transpose(state_ret, (0, 2, 3, 4, 1))
    w_in_t = jnp.transpose(w_in, (2, 0, 1))
    wo_bf = w_out.astype(BF16)
    wup_bf = w_up.astype(BF16)
    wdn_bf = w_down.astype(BF16)
    topk_s = min(TOPK_MAX, (past + dec_seq) // 4)

    yp = x_prompt.reshape(batch * seq, D_MODEL)
    ys = x_sample.reshape(nb, D_MODEL)
    kp, vp, kip, sp, kss, vss, kis, ss = [], [], [], [], [], [], [], []
    for l in range(depth):
        w_pack = _pack_w_in_t(w_in_t[:, l, :])
        wo, wup, wdn = wo_bf[l], wup_bf[l], wdn_bf[l]
        gpre = g_mix_pre[l].reshape(1, D_MODEL)

        (qa, ka, ka_bf, va, va_t, qi, kiwi, kiwi_bf, ki, qb, kb, vb, gb) = _proj_call(
            yp, gpre, w_pack, cos_p, sin_p, tm_p)
        a = _dsa_prompt_call(qa, qi, kiwi, ka_bf, va_t, kiwi_bf, batch, seq)
        bmix, s_new = _ret_prompt_call(qb, kb, vb, gb, g_ret[l], batch, seq)
        yp = _outffn_call(a, bmix, yp, wo, g_mix_post[l], g_ffn_pre[l], g_ffn_post[l], wup, wdn, 256)
        kp.append(ka); vp.append(va); kip.append(ki); sp.append(s_new)

        (qa, ka, ka_bf, va, va_t, qi, kiwi, kiwi_bf, ki, qb, kb, vb, gb) = _proj_call(
            ys, gpre, w_pack, cos_s, sin_s, nb)
        qi16 = jnp.pad(qi.reshape(nb, H_IDX, LANES)[:, :, :D_IDX], ((0, 0), (0, 16 - H_IDX), (0, 0)))
        wi16 = jnp.pad(kiwi[:, D_IDX:D_IDX + H_IDX], ((0, 0), (0, 16 - H_IDX)))[:, :, None]
        sc = _dsa_sample_scores_call(page_table, pool_ki_t, l, qi16, wi16, ki.reshape(nb, 1, D_IDX))
        sel = _dsa_sample_select_call(sc.reshape(nb, -1), topk_s, past + dec_seq)
        a = _dsa_sample_attn_call(page_table, pool_k_t, pool_v_t, l,
                                  qa.astype(F32).reshape(nb, 1, W_A),
                                  ka.reshape(nb, 1, W_A), va.reshape(nb, 1, W_A),
                                  sel.reshape(nb, 1, -1))
        bmix, s_new = _ret_sample_call(qb, kb, vb, gb, state_t, l, g_ret[l])
        ys = _outffn_call(a.reshape(nb, W_A).astype(BF16), bmix, ys, wo, g_mix_post[l], g_ffn_pre[l],
                          g_ffn_post[l], wup, wdn, nb)
        kss.append(ka); vss.append(va); kis.append(ki); ss.append(s_new)

    def heads(xs, lead):
        return jnp.stack(xs).reshape((depth,) + lead + (H_A, HEAD_DIM))

    ret_s = jnp.transpose(jnp.stack(ss), (0, 4, 1, 2, 3))
    return (yp.reshape(batch, seq, D_MODEL), ys.reshape(nb, dec_seq, D_MODEL),
            heads(kp, (batch, seq)), heads(vp, (batch, seq)),
            jnp.stack(kip).reshape(depth, batch, seq, D_IDX), jnp.stack(sp),
            heads(kss, (nb, dec_seq)), heads(vss, (nb, dec_seq)),
            jnp.stack(kis).reshape(depth, nb, dec_seq, D_IDX), ret_s)
```

```python
import functools

import numpy as np
import jax
import jax.numpy as jnp
from jax import lax
from jax.experimental import pallas as pl
from jax.experimental.pallas import tpu as pltpu

F32 = jnp.float32
BF16 = jnp.bfloat16
I32 = jnp.int32

D_MODEL = 1024
HEAD_DIM = 64
H_A = 8
H_B = 8
H_IDX = 4
D_IDX = 64
W_A = H_A * HEAD_DIM
W_B = H_B * HEAD_DIM
D_FF = 4 * D_MODEL
TOPK_MAX = 256
PAGE_SIZE = 128
ROPE_THETA = 10000.0
EPS = 1e-6
Q_BLOCK = 128
RET_CHUNK = 128
COL_SIZES = (W_A, W_A, W_A, H_IDX * D_IDX, D_IDX, H_IDX, W_B, W_B, W_B, W_B)

LANES = 128
NEG = -1e30
VMEM_LIMIT = 48 * 1024 * 1024

LOG2E = 1.4426950408889634
VT_ROWS = 144
VT_ALL = (H_A // 2) * VT_ROWS

NT = (((1,), (1,)), ((), ()))
TN = (((0,), (0,)), ((), ()))

C_QA, C_KA, C_VA, C_QI, C_KIWI, C_QB, C_KB, C_VB, C_GB = (
    0, 512, 1024, 1536, 2048, 2176, 2688, 3200, 3712)
N_PACK = 4224


def _cparams(sem):
    return pltpu.CompilerParams(dimension_semantics=sem, vmem_limit_bytes=VMEM_LIMIT)


def _rms(x, g):
    return x * lax.rsqrt(jnp.mean(x * x, axis=-1, keepdims=True) + EPS) * g


def _proj_kernel(x_ref, g_ref, w_ref, cos_ref, sin_ref,
                 qa_ref, ka_ref, kabf_ref, va_ref, vat_ref, qi_ref, kiwi_ref, kiwibf_ref,
                 ki_ref, qb_ref, kb_ref, vb_ref, gb_ref):
    x = x_ref[...]
    h = _rms(x, g_ref[...]).astype(BF16)
    cos = cos_ref[...]
    sin = sin_ref[...]
    lane = lax.broadcasted_iota(I32, cos.shape, 1)
    first_half = (lane % 64) < 32

    def rope(z, c, s):
        sw = jnp.where(first_half, pltpu.roll(z, 96, 1), pltpu.roll(z, 32, 1))
        return z * c + sw * s

    def proj(c0, n):
        return lax.dot_general(h, w_ref[c0:c0 + n, :], NT, preferred_element_type=F32)

    def roped(c0, scale):
        z = proj(c0, 512)
        parts = []
        for j in range(4):
            r = rope(z[:, j * LANES:(j + 1) * LANES], cos, sin)
            parts.append(r if scale is None else r * scale)
        return parts

    qa = roped(C_QA, HEAD_DIM ** -0.5 * LOG2E)
    for j in range(4):
        qa_ref[:, j * LANES:(j + 1) * LANES] = qa[j].astype(BF16)
    ka = roped(C_KA, None)
    for j in range(4):
        ka_ref[:, j * LANES:(j + 1) * LANES] = ka[j]
        kabf_ref[:, j * LANES:(j + 1) * LANES] = ka[j].astype(BF16)
    va = proj(C_VA, 512)
    va_ref[...] = va
    va_t = va.T
    ones = jnp.ones((VT_ROWS - LANES, va_t.shape[1]), BF16)
    for p in range(H_A // 2):
        vat_ref[p * VT_ROWS:p * VT_ROWS + LANES, :] = va_t[p * LANES:(p + 1) * LANES, :].astype(BF16)
        vat_ref[p * VT_ROWS + LANES:(p + 1) * VT_ROWS, :] = ones
    qi = roped(C_QI, None)
    for j in range(4):
        qi_ref[:, j * LANES:(j + 1) * LANES] = qi[j].astype(BF16)
    zk = proj(C_KIWI, LANES)
    is_key = lane < 64
    kiwi = rope(zk, jnp.where(is_key, cos, (H_IDX ** -0.5) * (D_IDX ** -0.5)),
                jnp.where(is_key, sin, 0.0))
    kiwi_ref[...] = kiwi
    kiwibf_ref[...] = kiwi.astype(BF16)
    ki_ref[...] = kiwi[:, 0:64]
    qb = roped(C_QB, None)
    for j in range(4):
        qb_ref[:, j * LANES:(j + 1) * LANES] = qb[j]
    kb = roped(C_KB, HEAD_DIM ** -0.5)
    for j in range(4):
        kb_ref[:, j * LANES:(j + 1) * LANES] = kb[j]
    vb_ref[...] = proj(C_VB, 512)
    gb_ref[...] = proj(C_GB, 512)


def _proj_call(x, g, w_pack_t, cos_t, sin_t, tm):
    m = x.shape[0]
    n_t = cos_t.shape[0] // tm
    row = lambda i: (i, 0)
    const = lambda i: (0, 0)
    tab = lambda i: (i % n_t, 0)
    wide = lambda dt: jax.ShapeDtypeStruct((m, 512), dt)
    out_shape = (wide(BF16), wide(F32), wide(BF16), wide(F32),
                 jax.ShapeDtypeStruct((m // tm, VT_ALL, tm), BF16), wide(BF16),
                 jax.ShapeDtypeStruct((m, LANES), F32), jax.ShapeDtypeStruct((m, LANES), BF16),
                 jax.ShapeDtypeStruct((m, 64), F32), wide(F32), wide(F32), wide(F32), wide(F32))
    out_specs = tuple(
        pl.BlockSpec((None, VT_ALL, tm), lambda i: (i, 0, 0)) if len(s.shape) == 3
        else pl.BlockSpec((tm, s.shape[1]), row) for s in out_shape)
    return pl.pallas_call(
        _proj_kernel, grid=(m // tm,),
        in_specs=[pl.BlockSpec((tm, D_MODEL), row),
                  pl.BlockSpec((1, D_MODEL), const),
                  pl.BlockSpec((N_PACK, D_MODEL), const, pipeline_mode=pl.Buffered(1)),
                  pl.BlockSpec((tm, LANES), tab),
                  pl.BlockSpec((tm, LANES), tab)],
        out_specs=out_specs, out_shape=out_shape,
        compiler_params=_cparams(("arbitrary",)), name="proj")(x, g, w_pack_t, cos_t, sin_t)


COUNT_ROWS = 32
KEY_NEG_INF = -2139095041
RANK_TOP = 65535
BIG = 3e38


def _key_to_float(key):
    bits = key ^ ((key >> 31) & 0x7FFFFFFF)
    return jnp.where(key <= KEY_NEG_INF, -jnp.inf, pltpu.bitcast(bits, F32))


def _count(scr, nc, ch, pred):
    def body(c, acc):
        m = jnp.where(pred(scr[c], c), 1.0, 0.0)
        for r in range(ch // COUNT_ROWS):
            acc = acc + m[r * COUNT_ROWS:(r + 1) * COUNT_ROWS]
        return acc

    acc = lax.fori_loop(0, nc, body, jnp.zeros((COUNT_ROWS, LANES), F32))
    return jnp.sum(acc, axis=0, keepdims=True)


def _bit_search(scr, nc, ch, target, init, nbits, cand_float):
    def bit_body(j, carry):
        lo, lo_cnt, up_cnt = carry
        cand = lo + jnp.left_shift(jnp.int32(1), nbits - 1 - j)
        cand_f = cand_float(cand)
        cnt = _count(scr, nc, ch, lambda s, c: s >= cand_f)
        ok = cnt >= target
        return jnp.where(ok, cand, lo), jnp.where(ok, cnt, lo_cnt), jnp.where(ok, up_cnt, cnt)

    return lax.fori_loop(0, nbits, bit_body, init)


def _topk_threshold(sc_scr, pos_scr, nc, ch, topk):
    assert ch >= topk and ch % COUNT_ROWS == 0
    kf = float(topk)
    total = (nc * ch).astype(F32) if hasattr(nc, "astype") else float(nc * ch)
    cnt0 = _count(sc_scr, nc, ch, lambda s, c: s >= 0.0)
    ok0 = cnt0 >= kf
    init = (jnp.where(ok0, 0, -32768), jnp.where(ok0, cnt0, total), jnp.where(ok0, 0.0, cnt0))
    t_hi, at_hi, above_hi = _bit_search(sc_scr, nc, ch, kf, init, 15,
                                        lambda v: _key_to_float(v << 16))
    need_lo = kf - above_hi
    bucket_n = at_hi - above_hi
    bucket_lo = _key_to_float(t_hi << 16)
    bucket_up = _key_to_float((t_hi + 1) << 16)

    def top_body(c, top):
        s = sc_scr[c]
        s_b = jnp.where((s >= bucket_lo) & jnp.logical_not(s >= bucket_up), s, -jnp.inf)
        for r in range(ch // COUNT_ROWS):
            top = jnp.maximum(top, s_b[r * COUNT_ROWS:(r + 1) * COUNT_ROWS])
        return top

    top = lax.fori_loop(0, nc, top_body, jnp.full((COUNT_ROWS, LANES), -jnp.inf, F32))
    v_top = jnp.max(top, axis=0, keepdims=True)
    n_top = _count(sc_scr, nc, ch, lambda s, c: s == v_top)
    one_key = (n_top == bucket_n) & (bucket_n > need_lo)
    krow = lax.broadcasted_iota(I32, (ch, LANES), 0)

    @pl.when(jnp.max(jnp.where(one_key, 1.0, 0.0)) > 0.0)
    def _():
        def rank_body(c, carry):
            s = sc_scr[c]
            rank = (RANK_TOP - (krow + c * ch)).astype(F32)
            ranked = jnp.where(s > v_top, BIG, jnp.where(s == v_top, rank, -BIG))
            sc_scr[c] = jnp.where(one_key, ranked, s)
            return carry

        lax.fori_loop(0, nc, rank_body, 0)

    def second_level(v):
        return jnp.where(one_key, v.astype(F32), _key_to_float((t_hi << 16) | v))

    t_lo, at_lo, above_lo = _bit_search(sc_scr, nc, ch, kf,
                                        (jnp.zeros_like(t_hi), at_hi, above_hi), 16, second_level)
    thr = second_level(t_lo)
    need_tie = kf - above_lo

    n_tied = at_lo - above_lo

    def tie_search():
        def tie_prep(c, carry):
            pos_scr[c] = jnp.where(sc_scr[c] == thr, (krow + c * ch).astype(F32), BIG)
            return carry

        lax.fori_loop(0, nc, tie_prep, 0)

        def tie_body(j, cut):
            cand = cut + jnp.left_shift(jnp.int32(1), 12 - j)
            cand_f = cand.astype(F32)
            f = _count(pos_scr, nc, ch, lambda p, c: p < cand_f)
            return jnp.where(f <= need_tie, cand, cut)

        return lax.fori_loop(0, 13, tie_body, jnp.zeros((1, LANES), I32))

    cut = lax.cond(jnp.max(n_tied - need_tie) > 0.0, tie_search,
                   lambda: jnp.full((1, LANES), 1 << 30, I32))
    return thr, cut


DSA_CH = 512


def _dsa_prompt_kernel(qa_ref, qi_ref, kiwi_ref, ka_ref, vat_ref, kibf_ref, o_ref,
                       sc_scr, pos_scr, bias_scr, acc_scr, qcat_scr, qicat_scr,
                       sa_scr, sb_scr, *, topk):
    ch = DSA_CH
    n_pair = H_A // 2
    i = pl.program_id(1)
    t0 = i * Q_BLOCK
    nc = lax.div(t0 + Q_BLOCK + ch - 1, ch)
    qpos = t0 + lax.broadcasted_iota(I32, (ch, LANES), 1)
    krow = lax.broadcasted_iota(I32, (ch, LANES), 0)
    wi_t = kiwi_ref[...].T

    d_lo = lax.broadcasted_iota(I32, (LANES, Q_BLOCK), 0) < HEAD_DIM
    for p in range(n_pair):
        qt = qa_ref[:, p * LANES:(p + 1) * LANES].astype(F32).T
        qcat_scr[p, :, 0:Q_BLOCK] = jnp.where(d_lo, qt, 0.0).astype(BF16)
        qcat_scr[p, :, Q_BLOCK:2 * Q_BLOCK] = jnp.where(d_lo, 0.0, qt).astype(BF16)
    for g in range(H_IDX // 2):
        for e in range(2):
            h = 2 * g + e
            qicat_scr[g, :, e * Q_BLOCK:(e + 1) * Q_BLOCK] = (
                qi_ref[:, h * LANES:(h + 1) * LANES].astype(F32).T.astype(BF16))

    def score_body(c, carry):
        off = pl.multiple_of(c * ch, ch)
        kic = kibf_ref[pl.ds(off, ch), :]
        acc = jnp.zeros((ch, LANES), F32)
        for g in range(H_IDX // 2):
            d = jnp.dot(kic, qicat_scr[g], preferred_element_type=F32)
            acc = (acc + wi_t[64 + 2 * g:65 + 2 * g, :] * jnp.maximum(d[:, 0:LANES], 0.0)
                   + wi_t[65 + 2 * g:66 + 2 * g, :] * jnp.maximum(d[:, LANES:2 * LANES], 0.0))
        acc = jnp.where(acc == 0.0, 0.0, acc)
        sc_scr[sel_index(c)] = jnp.where(krow + c * ch <= qpos, acc, -jnp.inf)
        return carry

    def sel_index(c):
        return (lax.shift_right_logical(c, 1), pl.ds(pl.multiple_of((c & 1) * ch, ch), ch))

    lax.fori_loop(0, nc, score_body, 0)

    odd = (nc & 1) == 1

    @pl.when(odd)
    def _():
        sc_scr[sel_index(nc)] = jnp.full((ch, LANES), -jnp.inf, F32)
        bias_scr[nc] = jnp.full((ch, LANES), NEG, F32)

    n2 = lax.shift_right_logical(nc + 1, 1)
    thr, cut = _topk_threshold(sc_scr, pos_scr, n2, 2 * ch, topk)

    def bias_body(c, carry):
        k = sc_scr[sel_index(c)]
        kpos = krow + c * ch
        sel = ((k > thr) | ((k == thr) & (kpos < cut))) & (kpos <= qpos)
        bias_scr[c] = jnp.where(sel, 0.0, NEG)
        return carry

    lax.fori_loop(0, nc, bias_body, 0)

    row_lo = lax.broadcasted_iota(I32, (LANES, Q_BLOCK), 0) < HEAD_DIM
    acc_scr[...] = jnp.zeros_like(acc_scr)
    last_chunk = ka_ref.shape[0] // ch - 1

    def qk_scores(c, p):
        off = pl.multiple_of(jnp.minimum(c, last_chunk) * ch, ch)
        return jnp.dot(ka_ref[pl.ds(off, ch), p * LANES:(p + 1) * LANES], qcat_scr[p],
                       preferred_element_type=F32)

    for p in range(n_pair):
        sa_scr[p] = qk_scores(0, p)

    def half_step(c, cur_scr, nxt_scr, carry):
        b = bias_scr[c]
        new = []
        for p in range(n_pair):
            nxt_scr[p] = qk_scores(c + 1, p)
        for p in range(n_pair):
            s = cur_scr[p]
            alphas, pts, ms = [], [], []
            for e in range(2):
                m = carry[4 * p + 2 * e]
                sb = s[:, e * LANES:(e + 1) * LANES] + b
                m_new = jnp.maximum(m, jnp.max(sb, axis=0, keepdims=True))
                alphas.append(jnp.exp2(m - m_new))
                pts.append(jnp.exp2(sb - m_new).astype(BF16))
                ms.append(m_new)
            pv = jnp.dot(vat_ref[c, p * VT_ROWS:(p + 1) * VT_ROWS, :],
                         jnp.concatenate(pts, axis=1), preferred_element_type=F32)
            for e in range(2):
                l = carry[4 * p + 2 * e + 1]
                new += [ms[e], alphas[e] * l + pv[LANES:LANES + 1, e * LANES:(e + 1) * LANES]]
            acc_scr[p] = (acc_scr[p] * jnp.where(row_lo, alphas[0], alphas[1])
                          + jnp.where(row_lo, pv[0:LANES, 0:LANES], pv[0:LANES, LANES:2 * LANES]))
        return tuple(new)

    def att_body(j, carry):
        carry = half_step(2 * j, sa_scr, sb_scr, carry)
        return half_step(2 * j + 1, sb_scr, sa_scr, carry)

    init = (jnp.full((1, LANES), NEG, F32), jnp.zeros((1, LANES), F32)) * H_A
    stats = lax.fori_loop(0, n2, att_body, init)
    for p in range(n_pair):
        l_pair = jnp.where(row_lo, stats[4 * p + 1], stats[4 * p + 3])
        o_ref[:, p * LANES:(p + 1) * LANES] = (acc_scr[p] / l_pair).T.astype(BF16)


def _dsa_prompt_call(qa, qi, kiwi, ka_bf, va_t, kiwi_bf, batch, seq):
    nq = seq // Q_BLOCK
    topk = min(TOPK_MAX, seq // 4)
    nch = seq // DSA_CH
    blk = lambda b, i: (b * nq + i, 0)
    full = lambda b, i: (b, 0)
    return pl.pallas_call(
        functools.partial(_dsa_prompt_kernel, topk=topk), grid=(batch, nq),
        in_specs=[pl.BlockSpec((Q_BLOCK, W_A), blk),
                  pl.BlockSpec((Q_BLOCK, 512), blk),
                  pl.BlockSpec((Q_BLOCK, LANES), blk),
                  pl.BlockSpec((seq, W_A), full),
                  pl.BlockSpec((nch, VT_ALL, DSA_CH), lambda b, i: (b, 0, 0)),
                  pl.BlockSpec((seq, LANES), full)],
        out_specs=pl.BlockSpec((Q_BLOCK, W_A), blk),
        out_shape=jax.ShapeDtypeStruct((batch * seq, W_A), BF16),
        scratch_shapes=[pltpu.VMEM((nch // 2, 2 * DSA_CH, LANES), F32),
                        pltpu.VMEM((nch // 2, 2 * DSA_CH, LANES), F32),
                        pltpu.VMEM((nch, DSA_CH, LANES), F32),
                        pltpu.VMEM((H_A // 2, LANES, Q_BLOCK), F32),
                        pltpu.VMEM((H_A // 2, LANES, 2 * Q_BLOCK), BF16),
                        pltpu.VMEM((H_IDX // 2, LANES, 2 * Q_BLOCK), BF16),
                        pltpu.VMEM((H_A // 2, DSA_CH, 2 * Q_BLOCK), F32),
                        pltpu.VMEM((H_A // 2, DSA_CH, 2 * Q_BLOCK), F32)],
        compiler_params=_cparams(("arbitrary", "arbitrary")),
        name="dsa_prompt")(qa, qi, kiwi, ka_bf, va_t, kiwi_bf)


def _swish(g):
    return g * (1.0 / (1.0 + jnp.exp(-g)))


def _ret_prompt_kernel(q_ref, k_ref, v_ref, g_ref, dmask_ref, qdec_ref, kdec_ref, cdec_ref,
                       gret_ref, avg_ref, o_ref, sout_ref, s_scr):
    c = pl.program_id(1)

    @pl.when(c == 0)
    def _():
        s_scr[...] = jnp.zeros_like(s_scr)

    lane_lo = lax.broadcasted_iota(I32, (RET_CHUNK, LANES), 1) < HEAD_DIM
    avg = avg_ref[...]
    same_head = (lax.broadcasted_iota(I32, (LANES, LANES), 0) < HEAD_DIM) == (
        lax.broadcasted_iota(I32, (LANES, LANES), 1) < HEAD_DIM)
    pairs = range(H_B // 2)
    cols = [slice(p * LANES, (p + 1) * LANES) for p in pairs]

    def head_mean(x):
        x_hi = x.astype(BF16)
        x_lo = (x - x_hi.astype(F32)).astype(BF16)
        return (jnp.dot(x_hi, avg, preferred_element_type=F32)
                + jnp.dot(x_lo, avg, preferred_element_type=F32))

    qb = [q_ref[:, cols[p]].astype(BF16) for p in pairs]
    kb = [k_ref[:, cols[p]].astype(BF16) for p in pairs]
    vb = [v_ref[:, cols[p]].astype(BF16) for p in pairs]
    att = []
    for p in pairs:
        zero = jnp.zeros_like(qb[p])
        for e, qe in enumerate((jnp.where(lane_lo, qb[p], zero), jnp.where(lane_lo, zero, qb[p]))):
            att.append(lax.dot_general(qe, kb[p], NT, preferred_element_type=F32) * dmask_ref[2 * p + e])
    o = []
    for p in pairs:
        intra = [jnp.dot(att[2 * p + e].astype(BF16), vb[p], preferred_element_type=F32)
                 for e in range(2)]
        s_bd = s_scr[p]
        o.append(jnp.where(lane_lo, intra[0], intra[1])
                 + jnp.dot(qb[p], s_bd.astype(BF16), preferred_element_type=F32) * qdec_ref[:, cols[p]])
        kd = (k_ref[:, cols[p]] * kdec_ref[:, cols[p]]).astype(BF16)
        upd = lax.dot_general(kd, vb[p], TN, preferred_element_type=F32)
        s_scr[p] = s_bd * cdec_ref[p] + jnp.where(same_head, upd, 0.0)
    dev = [o[p] - head_mean(o[p]) for p in pairs]
    var = [head_mean(dev[p] * dev[p]) for p in pairs]
    for p in pairs:
        on = dev[p] * lax.rsqrt(var[p] + EPS)
        o_ref[:, cols[p]] = (on * gret_ref[:, cols[p]] * _swish(g_ref[:, cols[p]])).astype(BF16)

    @pl.when(c == pl.num_programs(1) - 1)
    def _():
        for p in range(H_B // 2):
            s_bd = s_scr[p]
            sout_ref[2 * p] = s_bd[0:HEAD_DIM, 0:HEAD_DIM]
            sout_ref[2 * p + 1] = s_bd[HEAD_DIM:LANES, HEAD_DIM:LANES]


def _log_gamma():
    return jnp.log1p(-jnp.exp2(-5.0 - jnp.arange(H_B, dtype=F32)))


def _ret_tables(chunk):
    lg = _log_gamma()
    i = jnp.arange(chunk, dtype=F32)
    diff = i[:, None] - i[None, :]
    dmask = jnp.where(diff[None] >= 0, jnp.exp(jnp.maximum(diff, 0.0)[None] * lg[:, None, None]), 0.0)
    q_dec = jnp.exp((i[:, None] + 1.0) * lg[None, :])
    k_dec = jnp.exp((chunk - 1.0 - i)[:, None] * lg[None, :])
    c_dec = jnp.exp(chunk * lg)
    return dmask, q_dec, k_dec, c_dec


def _ret_prompt_call(qb, kb, vb, gb, g_ret, batch, seq):
    chunk = RET_CHUNK
    n = seq // chunk
    dmask, q_dec, k_dec, c_dec = _ret_tables(chunk)
    qdec = jnp.repeat(q_dec, HEAD_DIM, axis=1)
    kdec = jnp.repeat(k_dec, HEAD_DIM, axis=1)
    cdec = jnp.broadcast_to(jnp.repeat(c_dec, HEAD_DIM).reshape(H_B // 2, LANES, 1),
                            (H_B // 2, LANES, LANES))
    head_of = jnp.arange(LANES) // HEAD_DIM
    avg = jnp.where(head_of[:, None] == head_of[None, :], 1.0 / HEAD_DIM, 0.0).astype(BF16)
    blk = lambda b, c: (b * n + c, 0)
    c2 = lambda b, c: (0, 0)
    c3 = lambda b, c: (0, 0, 0)
    return pl.pallas_call(
        _ret_prompt_kernel, grid=(batch, n),
        in_specs=[pl.BlockSpec((chunk, W_B), blk)] * 4 + [
            pl.BlockSpec((H_B, chunk, chunk), c3),
            pl.BlockSpec((chunk, W_B), c2),
            pl.BlockSpec((chunk, W_B), c2),
            pl.BlockSpec((H_B // 2, LANES, LANES), c3),
            pl.BlockSpec((1, W_B), c2),
            pl.BlockSpec((LANES, LANES), c2)],
        out_specs=(pl.BlockSpec((chunk, W_B), blk),
                   pl.BlockSpec((None, H_B, HEAD_DIM, HEAD_DIM), lambda b, c: (b, 0, 0, 0))),
        out_shape=(jax.ShapeDtypeStruct((batch * seq, W_B), BF16),
                   jax.ShapeDtypeStruct((batch, H_B, HEAD_DIM, HEAD_DIM), F32)),
        scratch_shapes=[pltpu.VMEM((H_B // 2, LANES, LANES), F32)],
        compiler_params=_cparams(("arbitrary", "arbitrary")),
        name="ret_prompt")(qb, kb, vb, gb, dmask, qdec, kdec, cdec, g_ret.reshape(1, W_B), avg)


FF_CHUNK = 1024


def _outffn_kernel(a_ref, b_ref, x_ref, wo_ref, gpost_ref, gfpre_ref, gfpost_ref,
                   wup_ref, wdn_ref, y_ref):
    m = (jnp.dot(a_ref[...], wo_ref[0:W_A, :], preferred_element_type=F32)
         + jnp.dot(b_ref[...], wo_ref[W_A:D_MODEL, :], preferred_element_type=F32))
    x1 = x_ref[...] + _rms(m, gpost_ref[...])
    hn = _rms(x1, gfpre_ref[...]).astype(BF16)
    f = jnp.zeros(x1.shape, F32)
    for c in range(D_FF // FF_CHUNK):
        cs = slice(c * FF_CHUNK, (c + 1) * FF_CHUNK)
        u = jnp.dot(hn, wup_ref[:, cs], preferred_element_type=F32)
        u = jnp.square(jnp.maximum(u, 0.0)).astype(BF16)
        f = f + jnp.dot(u, wdn_ref[cs, :], preferred_element_type=F32)
    y_ref[...] = x1 + _rms(f, gfpost_ref[...])


def _outffn_call(a, b, x, wo, gpost, gfpre, gfpost, wup, wdn, tm):
    m = x.shape[0]
    row = lambda i: (i, 0)
    const = lambda i: (0, 0)
    once = dict(pipeline_mode=pl.Buffered(1))
    return pl.pallas_call(
        _outffn_kernel, grid=(m // tm,),
        in_specs=[pl.BlockSpec((tm, W_A), row), pl.BlockSpec((tm, W_B), row),
                  pl.BlockSpec((tm, D_MODEL), row),
                  pl.BlockSpec((D_MODEL, D_MODEL), const, **once),
                  pl.BlockSpec((1, D_MODEL), const), pl.BlockSpec((1, D_MODEL), const),
                  pl.BlockSpec((1, D_MODEL), const),
                  pl.BlockSpec((D_MODEL, D_FF), const, **once),
                  pl.BlockSpec((D_FF, D_MODEL), const, **once)],
        out_specs=pl.BlockSpec((tm, D_MODEL), row),
        out_shape=jax.ShapeDtypeStruct((m, D_MODEL), F32),
        compiler_params=_cparams(("arbitrary",)),
        name="outffn")(a, b, x, wo, gpost.reshape(1, -1), gfpre.reshape(1, -1),
                       gfpost.reshape(1, -1), wup, wdn)


def _dsa_sample_scores_kernel(pt_ref, *refs, n_pages):
    pages = refs[:n_pages]
    qi_ref, wi_ref, kinew_ref, o_ref = refs[n_pages:]
    qi = qi_ref[...]
    wi = wi_ref[...]

    def weighted(d):
        r = jnp.sum(wi * jnp.maximum(d, 0.0), axis=0, keepdims=True)
        return jnp.where(r == 0.0, 0.0, r)

    for p in range(n_pages):
        d = jnp.dot(qi, pages[p][...].astype(BF16), preferred_element_type=F32)
        o_ref[:, p * LANES:(p + 1) * LANES] = weighted(d)
    knew = kinew_ref[...].astype(BF16).astype(F32)
    d_new = jnp.sum(qi.astype(F32) * knew, axis=1, keepdims=True)
    lane = lax.broadcasted_iota(I32, (1, LANES), 1)
    o_ref[:, n_pages * LANES:(n_pages + 1) * LANES] = jnp.where(lane == 0, weighted(d_new), -jnp.inf)


def _dsa_sample_scores_call(page_table, pool_ki_t, layer, qi16, wi16, ki_new):
    nb, n_pages = page_table.shape
    ncol = (n_pages + 1) * LANES
    per = lambda b, pt: (b, 0, 0)
    page = lambda p: pl.BlockSpec((None, None, D_IDX, PAGE_SIZE),
                                  lambda b, pt: (layer, pt[b, p], 0, 0))
    return pl.pallas_call(
        functools.partial(_dsa_sample_scores_kernel, n_pages=n_pages),
        grid_spec=pltpu.PrefetchScalarGridSpec(
            num_scalar_prefetch=1, grid=(nb,),
            in_specs=[page(p) for p in range(n_pages)] + [
                pl.BlockSpec((None, 16, D_IDX), per),
                pl.BlockSpec((None, 16, 1), per),
                pl.BlockSpec((None, 1, D_IDX), per)],
            out_specs=pl.BlockSpec((None, 1, ncol), per)),
        out_shape=jax.ShapeDtypeStruct((nb, 1, ncol), F32),
        compiler_params=_cparams(("arbitrary",)),
        name="dsa_sample_scores")(page_table, *([pool_ki_t] * n_pages), qi16, wi16, ki_new)


def _dsa_sample_select_kernel(sc_ref, sel_ref, sc_scr, pos_scr, *, topk, n_valid):
    ncol = sc_ref.shape[1]
    for p in range(ncol // LANES):
        rows = slice(p * LANES, (p + 1) * LANES)
        sc_scr[0, rows, :] = sc_ref[:, rows].T
    thr, cut = _topk_threshold(sc_scr, pos_scr, 1, ncol, topk)
    krow = lax.broadcasted_iota(I32, (LANES, LANES), 0)
    for p in range(ncol // LANES):
        rows = slice(p * LANES, (p + 1) * LANES)
        k = sc_scr[0, rows, :]
        kpos = krow + p * LANES
        sel = ((k > thr) | ((k == thr) & (kpos < cut))) & (kpos < n_valid)
        sel_ref[:, rows] = jnp.where(sel, 1.0, 0.0).T


def _dsa_sample_select_call(sc, topk, n_valid):
    nb, ncol = sc.shape
    return pl.pallas_call(
        functools.partial(_dsa_sample_select_kernel, topk=topk, n_valid=n_valid),
        out_shape=jax.ShapeDtypeStruct((nb, ncol), F32),
        scratch_shapes=[pltpu.VMEM((1, ncol, nb), F32), pltpu.VMEM((1, ncol, nb), F32)],
        compiler_params=pltpu.CompilerParams(vmem_limit_bytes=VMEM_LIMIT),
        name="dsa_sample_select")(sc)


def _dsa_sample_attn_kernel(pt_ref, *refs, n_pages):
    kpages = refs[:n_pages]
    vpages = refs[n_pages:2 * n_pages]
    q_ref, knew_ref, vnew_ref, sel_ref, o_ref = refs[2 * n_pages:]
    rows = lax.broadcasted_iota(I32, (16, W_A), 0)
    lanes = lax.broadcasted_iota(I32, (16, W_A), 1)
    own_head = lax.shift_right_logical(lanes, 6) == rows
    q_bd = jnp.where(own_head, jnp.broadcast_to(q_ref[...], (16, W_A)), 0.0).astype(BF16)
    first_pos = lax.broadcasted_iota(I32, (PAGE_SIZE, W_A), 0) == 0

    def new_tile(ref):
        return jnp.where(first_pos, jnp.broadcast_to(ref[...], (PAGE_SIZE, W_A)), 0.0).astype(BF16)

    def page_t(ref):
        return ref[...].reshape(W_A, PAGE_SIZE).astype(BF16)

    s_all = []
    for p in range(n_pages + 1):
        if p < n_pages:
            s = jnp.dot(q_bd, page_t(kpages[p]), preferred_element_type=F32)
        else:
            s = lax.dot_general(q_bd, new_tile(knew_ref), NT, preferred_element_type=F32)
        s_all.append(jnp.where(sel_ref[:, p * LANES:(p + 1) * LANES] > 0.5, s, NEG))
    m = s_all[0]
    for s in s_all[1:]:
        m = jnp.maximum(m, s)
    m = jnp.max(m, axis=1, keepdims=True)
    p_all = [jnp.exp2(s - m) for s in s_all]
    l = p_all[0]
    for pp in p_all[1:]:
        l = l + pp
    inv = 1.0 / jnp.sum(l, axis=1, keepdims=True)
    acc = jnp.zeros((16, W_A), F32)
    for p in range(n_pages + 1):
        pn = (p_all[p] * inv).astype(BF16)
        if p < n_pages:
            acc = acc + lax.dot_general(pn, page_t(vpages[p]), NT, preferred_element_type=F32)
        else:
            acc = acc + jnp.dot(pn, new_tile(vnew_ref), preferred_element_type=F32)
    o_ref[...] = jnp.sum(jnp.where(own_head, acc, 0.0), axis=0, keepdims=True)


def _dsa_sample_attn_call(page_table, pool_k_t, pool_v_t, layer, q, k_new, v_new, sel):
    nb, n_pages = page_table.shape
    per = lambda b, pt: (b, 0, 0)
    page = lambda p: pl.BlockSpec((None, None, H_A, HEAD_DIM, PAGE_SIZE),
                                  lambda b, pt: (layer, pt[b, p], 0, 0, 0))
    row_blk = pl.BlockSpec((None, 1, W_A), per)
    return pl.pallas_call(
        functools.partial(_dsa_sample_attn_kernel, n_pages=n_pages),
        grid_spec=pltpu.PrefetchScalarGridSpec(
            num_scalar_prefetch=1, grid=(nb,),
            in_specs=[page(p) for p in range(n_pages)] * 2 + [
                row_blk, row_blk, row_blk,
                pl.BlockSpec((None, 1, sel.shape[2]), per)],
            out_specs=row_blk),
        out_shape=jax.ShapeDtypeStruct((nb, 1, W_A), F32),
        compiler_params=_cparams(("arbitrary",)),
        name="dsa_sample_attn")(page_table, *([pool_k_t] * n_pages), *([pool_v_t] * n_pages),
                                q, k_new, v_new, sel)


def _ret_sample_kernel(q_ref, k_ref, v_ref, g_ref, s_ref, gret_ref, gam_ref, o_ref, sout_ref):
    v = v_ref[...]
    gam = gam_ref[...]

    def body(d, o):
        s_new = s_ref[d] * gam + k_ref[pl.ds(d, 1), :] * v
        sout_ref[d] = s_new
        return o + q_ref[pl.ds(d, 1), :] * s_new

    o = lax.fori_loop(0, HEAD_DIM, body, jnp.zeros(v.shape, F32))
    mu = jnp.mean(o, axis=0, keepdims=True)
    var = jnp.mean(jnp.square(o - mu), axis=0, keepdims=True)
    on = (o - mu) * lax.rsqrt(var + EPS)
    o_ref[...] = on * gret_ref[...] * _swish(g_ref[...])


def _ret_sample_call(qb, kb, vb, gb, state_t, layer, g_ret):
    nb = qb.shape[0]
    gam = jnp.exp(1.0 * _log_gamma())
    gam_b = jnp.broadcast_to(jnp.repeat(gam, HEAD_DIM)[:, None], (W_B, nb))
    gret_b = jnp.broadcast_to(g_ret[:, None], (W_B, nb))
    head = pl.BlockSpec((HEAD_DIM, nb), lambda h: (h, 0))
    o_t, s_new = pl.pallas_call(
        _ret_sample_kernel, grid=(H_B,),
        in_specs=[head, head, head, head,
                  pl.BlockSpec((None, None, HEAD_DIM, HEAD_DIM, nb), lambda h: (layer, h, 0, 0, 0)),
                  head, head],
        out_specs=(head, pl.BlockSpec((None, HEAD_DIM, HEAD_DIM, nb), lambda h: (h, 0, 0, 0))),
        out_shape=(jax.ShapeDtypeStruct((W_B, nb), F32),
                   jax.ShapeDtypeStruct((H_B, HEAD_DIM, HEAD_DIM, nb), F32)),
        compiler_params=_cparams(("arbitrary",)),
        name="ret_sample")(qb.T, kb.T, vb.T, gb.T, state_t, gret_b, gam_b)
    return o_t.T.astype(BF16), s_new


def _pack_w_in_t(w_t):
    d = w_t.shape[1]
    splits = np.cumsum(COL_SIZES)[:-1].tolist()
    qa, ka, va, qi, ki, wi, qb, kb, vb, gb = jnp.split(w_t, splits, axis=0)
    qi_pad = jnp.pad(qi.reshape(H_IDX, D_IDX, d), ((0, 0), (0, LANES - D_IDX), (0, 0))).reshape(-1, d)
    kiwi = jnp.concatenate([ki, wi, jnp.zeros((LANES - D_IDX - H_IDX, d), w_t.dtype)], axis=0)
    return jnp.concatenate([qa, ka, va, qi_pad, kiwi, qb, kb, vb, gb], axis=0).astype(BF16)


def _rope_tables(pos):
    half = HEAD_DIM // 2
    inv = ROPE_THETA ** (-jnp.arange(half, dtype=F32) / half)
    ang = pos[:, None] * inv[None, :]
    c = jnp.cos(ang)
    s = jnp.sin(ang)
    return jnp.concatenate([c, c, c, c], axis=1), jnp.concatenate([-s, s, -s, s], axis=1)


def kernel(x_prompt, x_sample, cache_k, cache_v, cache_kidx, state_ret, page_table, w_in, w_out,
           g_ret, g_mix_pre, g_mix_post, g_ffn_pre, g_ffn_post, w_up, w_down):
    batch, seq, _ = x_prompt.shape
    nb, dec_seq, _ = x_sample.shape
    depth = w_in.shape[0]
    n_pages = page_table.shape[1]
    past = n_pages * PAGE_SIZE
    assert dec_seq == 1 and seq % DSA_CH == 0 and nb == LANES
    tm_p = DSA_CH

    cos_p, sin_p = _rope_tables(jnp.arange(seq, dtype=F32))
    cos_s, sin_s = _rope_tables(past + jnp.arange(dec_seq, dtype=F32))
    cos_s = jnp.broadcast_to(cos_s, (nb, LANES))
    sin_s = jnp.broadcast_to(sin_s, (nb, LANES))
    pool_k_t = jnp.transpose(cache_k, (0, 1, 3, 4, 2))
    pool_v_t = jnp.transpose(cache_v, (0, 1, 3, 4, 2))
    pool_ki_t = jnp.swapaxes(cache_kidx, 2, 3)
    state_t = jnp.transpose(state_ret, (0, 2, 3, 4, 1))
    w_in_t = jnp.transpose(w_in, (2, 0, 1))
    wo_bf = w_out.astype(BF16)
    wup_bf = w_up.astype(BF16)
    wdn_bf = w_down.astype(BF16)
    topk_s = min(TOPK_MAX, (past + dec_seq) // 4)

    yp = x_prompt.reshape(batch * seq, D_MODEL)
    ys = x_sample.reshape(nb, D_MODEL)
    kp, vp, kip, sp, kss, vss, kis, ss = [], [], [], [], [], [], [], []
    for l in range(depth):
        w_pack = _pack_w_in_t(w_in_t[:, l, :])
        wo, wup, wdn = wo_bf[l], wup_bf[l], wdn_bf[l]
        gpre = g_mix_pre[l].reshape(1, D_MODEL)

        (qa, ka, ka_bf, va, va_t, qi, kiwi, kiwi_bf, ki, qb, kb, vb, gb) = _proj_call(
            yp, gpre, w_pack, cos_p, sin_p, tm_p)
        a = _dsa_prompt_call(qa, qi, kiwi, ka_bf, va_t, kiwi_bf, batch, seq)
        bmix, s_new = _ret_prompt_call(qb, kb, vb, gb, g_ret[l], batch, seq)
        yp = _outffn_call(a, bmix, yp, wo, g_mix_post[l], g_ffn_pre[l], g_ffn_post[l], wup, wdn, 512)
        kp.append(ka); vp.append(va); kip.append(ki); sp.append(s_new)

        (qa, ka, ka_bf, va, va_t, qi, kiwi, kiwi_bf, ki, qb, kb, vb, gb) = _proj_call(
            ys, gpre, w_pack, cos_s, sin_s, nb)
        qi16 = jnp.pad(qi.reshape(nb, H_IDX, LANES)[:, :, :D_IDX], ((0, 0), (0, 16 - H_IDX), (0, 0)))
        wi16 = jnp.pad(kiwi[:, D_IDX:D_IDX + H_IDX], ((0, 0), (0, 16 - H_IDX)))[:, :, None]
        sc = _dsa_sample_scores_call(page_table, pool_ki_t, l, qi16, wi16, ki.reshape(nb, 1, D_IDX))
        sel = _dsa_sample_select_call(sc.reshape(nb, -1), topk_s, past + dec_seq)
        a = _dsa_sample_attn_call(page_table, pool_k_t, pool_v_t, l,
                                  qa.astype(F32).reshape(nb, 1, W_A),
                                  ka.reshape(nb, 1, W_A), va.reshape(nb, 1, W_A),
                                  sel.reshape(nb, 1, -1))
        bmix, s_new = _ret_sample_call(qb, kb, vb, gb, state_t, l, g_ret[l])
        ys = _outffn_call(a.reshape(nb, W_A).astype(BF16), bmix, ys, wo, g_mix_post[l], g_ffn_pre[l],
                          g_ffn_post[l], wup, wdn, nb)
        kss.append(ka); vss.append(va); kis.append(ki); ss.append(s_new)

    def heads(xs, lead):
        return jnp.stack(xs).reshape((depth,) + lead + (H_A, HEAD_DIM))

    ret_s = jnp.transpose(jnp.stack(ss), (0, 4, 1, 2, 3))
    return (yp.reshape(batch, seq, D_MODEL), ys.reshape(nb, dec_seq, D_MODEL),
            heads(kp, (batch, seq)), heads(vp, (batch, seq)),
            jnp.stack(kip).reshape(depth, batch, seq, D_IDX), jnp.stack(sp),
            heads(kss, (nb, dec_seq)), heads(vss, (nb, dec_seq)),
            jnp.stack(kis).reshape(depth, nb, dec_seq, D_IDX), ret_s)
```

```python
import functools

import numpy as np
import jax
import jax.numpy as jnp
from jax import lax
from jax.experimental import pallas as pl
from jax.experimental.pallas import tpu as pltpu

F32 = jnp.float32
BF16 = jnp.bfloat16
I32 = jnp.int32

D_MODEL = 1024
HEAD_DIM = 64
H_A = 8
H_B = 8
H_IDX = 4
D_IDX = 64
W_A = H_A * HEAD_DIM
W_B = H_B * HEAD_DIM
D_FF = 4 * D_MODEL
TOPK_MAX = 256
PAGE_SIZE = 128
ROPE_THETA = 10000.0
EPS = 1e-6
Q_BLOCK = 128
RET_CHUNK = 128
COL_SIZES = (W_A, W_A, W_A, H_IDX * D_IDX, D_IDX, H_IDX, W_B, W_B, W_B, W_B)

LANES = 128
NEG = -1e30
VMEM_LIMIT = 48 * 1024 * 1024

LOG2E = 1.4426950408889634
VT_ROWS = 144
VT_ALL = (H_A // 2) * VT_ROWS

NT = (((1,), (1,)), ((), ()))
TN = (((0,), (0,)), ((), ()))

C_QA, C_KA, C_VA, C_QI, C_KIWI, C_QB, C_KB, C_VB, C_GB = (
    0, 512, 1024, 1536, 2048, 2176, 2688, 3200, 3712)
N_PACK = 4224


def _cparams(sem):
    return pltpu.CompilerParams(dimension_semantics=sem, vmem_limit_bytes=VMEM_LIMIT)


def _rms(x, g):
    return x * lax.rsqrt(jnp.mean(x * x, axis=-1, keepdims=True) + EPS) * g


def _proj_kernel(x_ref, g_ref, whead_ref, wmid_ref, wtail_ref, cos_ref, sin_ref,
                 qa_ref, ka_ref, kabf_ref, va_ref, vat_ref, qi_ref, kiwi_ref, kiwibf_ref,
                 ki_ref, qb_ref, kb_ref, vb_ref, gb_ref):
    x = x_ref[...]
    h = _rms(x, g_ref[...]).astype(BF16)
    cos = cos_ref[...]
    sin = sin_ref[...]
    lane = lax.broadcasted_iota(I32, cos.shape, 1)
    first_half = (lane % 64) < 32

    def rope(z, c, s):
        sw = jnp.where(first_half, pltpu.roll(z, 96, 1), pltpu.roll(z, 32, 1))
        return z * c + sw * s

    def proj(c0, n):
        if c0 < C_QI:
            w = whead_ref[c0:c0 + n, :]
        elif c0 < C_QB:
            w = wmid_ref[c0 - C_QI:c0 - C_QI + n, :]
        else:
            w = wtail_ref[c0 - C_QB:c0 - C_QB + n, :]
        return lax.dot_general(h, w, NT, preferred_element_type=F32)

    def roped(c0, scale):
        z = proj(c0, 512)
        parts = []
        for j in range(4):
            r = rope(z[:, j * LANES:(j + 1) * LANES], cos, sin)
            parts.append(r if scale is None else r * scale)
        return parts

    qa = roped(C_QA, HEAD_DIM ** -0.5 * LOG2E)
    for j in range(4):
        qa_ref[:, j * LANES:(j + 1) * LANES] = qa[j].astype(BF16)
    ka = roped(C_KA, None)
    for j in range(4):
        ka_ref[:, j * LANES:(j + 1) * LANES] = ka[j]
        kabf_ref[:, j * LANES:(j + 1) * LANES] = ka[j].astype(BF16)
    va = proj(C_VA, 512)
    va_ref[...] = va
    va_t = va.T
    ones = jnp.ones((VT_ROWS - LANES, va_t.shape[1]), BF16)
    for p in range(H_A // 2):
        vat_ref[p * VT_ROWS:p * VT_ROWS + LANES, :] = va_t[p * LANES:(p + 1) * LANES, :].astype(BF16)
        vat_ref[p * VT_ROWS + LANES:(p + 1) * VT_ROWS, :] = ones
    qi = roped(C_QI, None)
    for j in range(4):
        qi_ref[:, j * LANES:(j + 1) * LANES] = qi[j].astype(BF16)
    zk = proj(C_KIWI, LANES)
    is_key = lane < 64
    kiwi = rope(zk, jnp.where(is_key, cos, (H_IDX ** -0.5) * (D_IDX ** -0.5)),
                jnp.where(is_key, sin, 0.0))
    kiwi_ref[...] = kiwi
    kiwibf_ref[...] = kiwi.astype(BF16)
    ki_ref[...] = kiwi[:, 0:64]
    qb = roped(C_QB, None)
    for j in range(4):
        qb_ref[:, j * LANES:(j + 1) * LANES] = qb[j]
    kb = roped(C_KB, HEAD_DIM ** -0.5)
    for j in range(4):
        kb_ref[:, j * LANES:(j + 1) * LANES] = kb[j]
    vb_ref[...] = proj(C_VB, 512)
    gb_ref[...] = proj(C_GB, 512)


def _proj_call(x, g, w_t, w_mid, w_tail, cos_t, sin_t, tm):
    m = x.shape[0]
    once = dict(pipeline_mode=pl.Buffered(1))
    n_t = cos_t.shape[0] // tm
    row = lambda i: (i, 0)
    const = lambda i: (0, 0)
    tab = lambda i: (i % n_t, 0)
    wide = lambda dt: jax.ShapeDtypeStruct((m, 512), dt)
    out_shape = (wide(BF16), wide(F32), wide(BF16), wide(F32),
                 jax.ShapeDtypeStruct((m // tm, VT_ALL, tm), BF16), wide(BF16),
                 jax.ShapeDtypeStruct((m, LANES), F32), jax.ShapeDtypeStruct((m, LANES), BF16),
                 jax.ShapeDtypeStruct((m, 64), F32), wide(F32), wide(F32), wide(F32), wide(F32))
    out_specs = tuple(
        pl.BlockSpec((None, VT_ALL, tm), lambda i: (i, 0, 0)) if len(s.shape) == 3
        else pl.BlockSpec((tm, s.shape[1]), row) for s in out_shape)
    return pl.pallas_call(
        _proj_kernel, grid=(m // tm,),
        in_specs=[pl.BlockSpec((tm, D_MODEL), row),
                  pl.BlockSpec((1, D_MODEL), const),
                  pl.BlockSpec((C_QI, D_MODEL), const, **once),
                  pl.BlockSpec((C_QB - C_QI, D_MODEL), const, **once),
                  pl.BlockSpec((N_PACK - C_QB, D_MODEL), const, **once),
                  pl.BlockSpec((tm, LANES), tab),
                  pl.BlockSpec((tm, LANES), tab)],
        out_specs=out_specs, out_shape=out_shape,
        compiler_params=_cparams(("arbitrary",)), name="proj")(x, g, w_t, w_mid, w_tail, cos_t, sin_t)


COUNT_ROWS = 32
KEY_NEG_INF = -2139095041
RANK_TOP = 65535
BIG = 3e38


def _key_to_float(key):
    bits = key ^ ((key >> 31) & 0x7FFFFFFF)
    return jnp.where(key <= KEY_NEG_INF, -jnp.inf, pltpu.bitcast(bits, F32))


def _count(scr, nc, ch, pred):
    def body(c, acc):
        m = jnp.where(pred(scr[c], c), 1.0, 0.0)
        for r in range(ch // COUNT_ROWS):
            acc = acc + m[r * COUNT_ROWS:(r + 1) * COUNT_ROWS]
        return acc

    acc = lax.fori_loop(0, nc, body, jnp.zeros((COUNT_ROWS, LANES), F32))
    return jnp.sum(acc, axis=0, keepdims=True)


def _bit_search(scr, nc, ch, target, init, nbits, cand_float):
    def bit_body(j, carry):
        lo, lo_cnt, up_cnt = carry
        cand = lo + jnp.left_shift(jnp.int32(1), nbits - 1 - j)
        cand_f = cand_float(cand)
        cnt = _count(scr, nc, ch, lambda s, c: s >= cand_f)
        ok = cnt >= target
        return jnp.where(ok, cand, lo), jnp.where(ok, cnt, lo_cnt), jnp.where(ok, up_cnt, cnt)

    return lax.fori_loop(0, nbits, bit_body, init)


def _topk_threshold(sc_scr, pos_scr, nc, ch, topk):
    assert ch >= topk and ch % COUNT_ROWS == 0
    kf = float(topk)
    total = (nc * ch).astype(F32) if hasattr(nc, "astype") else float(nc * ch)
    cnt0 = _count(sc_scr, nc, ch, lambda s, c: s >= 0.0)
    ok0 = cnt0 >= kf
    init = (jnp.where(ok0, 0, -32768), jnp.where(ok0, cnt0, total), jnp.where(ok0, 0.0, cnt0))
    t_hi, at_hi, above_hi = _bit_search(sc_scr, nc, ch, kf, init, 15,
                                        lambda v: _key_to_float(v << 16))
    need_lo = kf - above_hi
    bucket_n = at_hi - above_hi
    bucket_lo = _key_to_float(t_hi << 16)
    bucket_up = _key_to_float((t_hi + 1) << 16)

    def top_body(c, top):
        s = sc_scr[c]
        s_b = jnp.where((s >= bucket_lo) & jnp.logical_not(s >= bucket_up), s, -jnp.inf)
        for r in range(ch // COUNT_ROWS):
            top = jnp.maximum(top, s_b[r * COUNT_ROWS:(r + 1) * COUNT_ROWS])
        return top

    top = lax.fori_loop(0, nc, top_body, jnp.full((COUNT_ROWS, LANES), -jnp.inf, F32))
    v_top = jnp.max(top, axis=0, keepdims=True)
    n_top = _count(sc_scr, nc, ch, lambda s, c: s == v_top)
    one_key = (n_top == bucket_n) & (bucket_n > need_lo)
    krow = lax.broadcasted_iota(I32, (ch, LANES), 0)

    @pl.when(jnp.max(jnp.where(one_key, 1.0, 0.0)) > 0.0)
    def _():
        def rank_body(c, carry):
            s = sc_scr[c]
            rank = (RANK_TOP - (krow + c * ch)).astype(F32)
            ranked = jnp.where(s > v_top, BIG, jnp.where(s == v_top, rank, -BIG))
            sc_scr[c] = jnp.where(one_key, ranked, s)
            return carry

        lax.fori_loop(0, nc, rank_body, 0)

    def second_level(v):
        return jnp.where(one_key, v.astype(F32), _key_to_float((t_hi << 16) | v))

    t_lo, at_lo, above_lo = _bit_search(sc_scr, nc, ch, kf,
                                        (jnp.zeros_like(t_hi), at_hi, above_hi), 16, second_level)
    thr = second_level(t_lo)
    need_tie = kf - above_lo

    n_tied = at_lo - above_lo

    def tie_search():
        def tie_prep(c, carry):
            pos_scr[c] = jnp.where(sc_scr[c] == thr, (krow + c * ch).astype(F32), BIG)
            return carry

        lax.fori_loop(0, nc, tie_prep, 0)

        def tie_body(j, cut):
            cand = cut + jnp.left_shift(jnp.int32(1), 12 - j)
            cand_f = cand.astype(F32)
            f = _count(pos_scr, nc, ch, lambda p, c: p < cand_f)
            return jnp.where(f <= need_tie, cand, cut)

        return lax.fori_loop(0, 13, tie_body, jnp.zeros((1, LANES), I32))

    cut = lax.cond(jnp.max(n_tied - need_tie) > 0.0, tie_search,
                   lambda: jnp.full((1, LANES), 1 << 30, I32))
    return thr, cut


DSA_CH = 512


def _dsa_prompt_kernel(qa_ref, qi_ref, kiwi_ref, ka_ref, vat_ref, kibf_ref, o_ref,
                       sc_scr, pos_scr, bias_scr, acc_scr, qcat_scr, qicat_scr,
                       sa_scr, sb_scr, *, topk):
    ch = DSA_CH
    n_pair = H_A // 2
    i = pl.program_id(1)
    t0 = i * Q_BLOCK
    nc = lax.div(t0 + Q_BLOCK + ch - 1, ch)
    qpos = t0 + lax.broadcasted_iota(I32, (ch, LANES), 1)
    krow = lax.broadcasted_iota(I32, (ch, LANES), 0)
    wi_t = kiwi_ref[...].T

    d_lo = lax.broadcasted_iota(I32, (LANES, Q_BLOCK), 0) < HEAD_DIM
    for p in range(n_pair):
        qt = qa_ref[:, p * LANES:(p + 1) * LANES].astype(F32).T
        qcat_scr[p, :, 0:Q_BLOCK] = jnp.where(d_lo, qt, 0.0).astype(BF16)
        qcat_scr[p, :, Q_BLOCK:2 * Q_BLOCK] = jnp.where(d_lo, 0.0, qt).astype(BF16)
    for g in range(H_IDX // 2):
        for e in range(2):
            h = 2 * g + e
            qicat_scr[g, :, e * Q_BLOCK:(e + 1) * Q_BLOCK] = (
                qi_ref[:, h * LANES:(h + 1) * LANES].astype(F32).T.astype(BF16))

    def score_body(c, carry):
        off = pl.multiple_of(c * ch, ch)
        kic = kibf_ref[pl.ds(off, ch), :]
        acc = jnp.zeros((ch, LANES), F32)
        for g in range(H_IDX // 2):
            d = jnp.dot(kic, qicat_scr[g], preferred_element_type=F32)
            acc = (acc + wi_t[64 + 2 * g:65 + 2 * g, :] * jnp.maximum(d[:, 0:LANES], 0.0)
                   + wi_t[65 + 2 * g:66 + 2 * g, :] * jnp.maximum(d[:, LANES:2 * LANES], 0.0))
        acc = jnp.where(acc == 0.0, 0.0, acc)
        sc_scr[sel_index(c)] = jnp.where(krow + c * ch <= qpos, acc, -jnp.inf)
        return carry

    def sel_index(c):
        return (lax.shift_right_logical(c, 1), pl.ds(pl.multiple_of((c & 1) * ch, ch), ch))

    lax.fori_loop(0, nc, score_body, 0)

    odd = (nc & 1) == 1

    @pl.when(odd)
    def _():
        sc_scr[sel_index(nc)] = jnp.full((ch, LANES), -jnp.inf, F32)

    n2 = lax.shift_right_logical(nc + 1, 1)
    thr, cut = _topk_threshold(sc_scr, pos_scr, n2, 2 * ch, topk)

    def bias_body(c, carry):
        k = sc_scr[sel_index(c)]
        kpos = krow + c * ch
        sel = ((k > thr) | ((k == thr) & (kpos < cut))) & (kpos <= qpos)
        bias_scr[c] = jnp.where(sel, 0.0, NEG)
        return carry

    lax.fori_loop(0, nc, bias_body, 0)

    row_lo = lax.broadcasted_iota(I32, (LANES, Q_BLOCK), 0) < HEAD_DIM
    acc_scr[...] = jnp.zeros_like(acc_scr)
    last_chunk = ka_ref.shape[0] // ch - 1

    def qk_scores(c, p):
        off = pl.multiple_of(jnp.minimum(c, last_chunk) * ch, ch)
        return jnp.dot(ka_ref[pl.ds(off, ch), p * LANES:(p + 1) * LANES], qcat_scr[p],
                       preferred_element_type=F32)

    for p in range(n_pair):
        sa_scr[p] = qk_scores(0, p)

    def half_step(c, cur_scr, nxt_scr, carry, prefetch=True):
        b = bias_scr[c]
        new = []
        if prefetch:
            for p in range(n_pair):
                nxt_scr[p] = qk_scores(c + 1, p)
        for p in range(n_pair):
            s = cur_scr[p]
            alphas, pts, ms = [], [], []
            for e in range(2):
                m = carry[4 * p + 2 * e]
                sb = s[:, e * LANES:(e + 1) * LANES] + b
                m_new = jnp.maximum(m, jnp.max(sb, axis=0, keepdims=True))
                alphas.append(jnp.exp2(m - m_new))
                pts.append(jnp.exp2(sb - m_new).astype(BF16))
                ms.append(m_new)
            pv = jnp.dot(vat_ref[c, p * VT_ROWS:(p + 1) * VT_ROWS, :],
                         jnp.concatenate(pts, axis=1), preferred_element_type=F32)
            for e in range(2):
                l = carry[4 * p + 2 * e + 1]
                new += [ms[e], alphas[e] * l + pv[LANES:LANES + 1, e * LANES:(e + 1) * LANES]]
            acc_scr[p] = (acc_scr[p] * jnp.where(row_lo, alphas[0], alphas[1])
                          + jnp.where(row_lo, pv[0:LANES, 0:LANES], pv[0:LANES, LANES:2 * LANES]))
        return tuple(new)

    def att_body(j, carry):
        carry = half_step(2 * j, sa_scr, sb_scr, carry)
        return half_step(2 * j + 1, sb_scr, sa_scr, carry)

    init = (jnp.full((1, LANES), NEG, F32), jnp.zeros((1, LANES), F32)) * H_A
    stats = lax.fori_loop(0, lax.shift_right_logical(nc, 1), att_body, init)
    stats = lax.cond(odd, lambda st: half_step(nc - 1, sa_scr, sb_scr, st, prefetch=False),
                     lambda st: st, stats)
    for p in range(n_pair):
        l_pair = jnp.where(row_lo, stats[4 * p + 1], stats[4 * p + 3])
        o_ref[:, p * LANES:(p + 1) * LANES] = (acc_scr[p] / l_pair).T.astype(BF16)


def _dsa_prompt_call(qa, qi, kiwi, ka_bf, va_t, kiwi_bf, batch, seq):
    nq = seq // Q_BLOCK
    topk = min(TOPK_MAX, seq // 4)
    nch = seq // DSA_CH
    blk = lambda b, i: (b * nq + i, 0)
    full = lambda b, i: (b, 0)
    return pl.pallas_call(
        functools.partial(_dsa_prompt_kernel, topk=topk), grid=(batch, nq),
        in_specs=[pl.BlockSpec((Q_BLOCK, W_A), blk),
                  pl.BlockSpec((Q_BLOCK, 512), blk),
                  pl.BlockSpec((Q_BLOCK, LANES), blk),
                  pl.BlockSpec((seq, W_A), full),
                  pl.BlockSpec((nch, VT_ALL, DSA_CH), lambda b, i: (b, 0, 0)),
                  pl.BlockSpec((seq, LANES), full)],
        out_specs=pl.BlockSpec((Q_BLOCK, W_A), blk),
        out_shape=jax.ShapeDtypeStruct((batch * seq, W_A), BF16),
        scratch_shapes=[pltpu.VMEM((nch // 2, 2 * DSA_CH, LANES), F32),
                        pltpu.VMEM((nch // 2, 2 * DSA_CH, LANES), F32),
                        pltpu.VMEM((nch, DSA_CH, LANES), F32),
                        pltpu.VMEM((H_A // 2, LANES, Q_BLOCK), F32),
                        pltpu.VMEM((H_A // 2, LANES, 2 * Q_BLOCK), BF16),
                        pltpu.VMEM((H_IDX // 2, LANES, 2 * Q_BLOCK), BF16),
                        pltpu.VMEM((H_A // 2, DSA_CH, 2 * Q_BLOCK), F32),
                        pltpu.VMEM((H_A // 2, DSA_CH, 2 * Q_BLOCK), F32)],
        compiler_params=_cparams(("arbitrary", "arbitrary")),
        name="dsa_prompt")(qa, qi, kiwi, ka_bf, va_t, kiwi_bf)


def _swish(g):
    return g * (1.0 / (1.0 + jnp.exp(-g)))


def _ret_prompt_kernel(q_ref, k_ref, v_ref, g_ref, dmask_ref, qdec_ref, kdec_ref, cdec_ref,
                       gret_ref, avg_ref, o_ref, sout_ref, s_scr):
    c = pl.program_id(1)

    @pl.when(c == 0)
    def _():
        s_scr[...] = jnp.zeros_like(s_scr)

    lane_lo = lax.broadcasted_iota(I32, (RET_CHUNK, LANES), 1) < HEAD_DIM
    avg = avg_ref[...]
    same_head = (lax.broadcasted_iota(I32, (LANES, LANES), 0) < HEAD_DIM) == (
        lax.broadcasted_iota(I32, (LANES, LANES), 1) < HEAD_DIM)
    pairs = range(H_B // 2)
    cols = [slice(p * LANES, (p + 1) * LANES) for p in pairs]

    def head_mean(x):
        x_hi = x.astype(BF16)
        x_lo = (x - x_hi.astype(F32)).astype(BF16)
        return (jnp.dot(x_hi, avg, preferred_element_type=F32)
                + jnp.dot(x_lo, avg, preferred_element_type=F32))

    qb = [q_ref[:, cols[p]].astype(BF16) for p in pairs]
    kb = [k_ref[:, cols[p]].astype(BF16) for p in pairs]
    vb = [v_ref[:, cols[p]].astype(BF16) for p in pairs]
    att = []
    for p in pairs:
        zero = jnp.zeros_like(qb[p])
        for e, qe in enumerate((jnp.where(lane_lo, qb[p], zero), jnp.where(lane_lo, zero, qb[p]))):
            att.append(lax.dot_general(qe, kb[p], NT, preferred_element_type=F32) * dmask_ref[2 * p + e])
    o = []
    for p in pairs:
        intra = [jnp.dot(att[2 * p + e].astype(BF16), vb[p], preferred_element_type=F32)
                 for e in range(2)]
        s_bd = s_scr[p]
        o.append(jnp.where(lane_lo, intra[0], intra[1])
                 + jnp.dot(qb[p], s_bd.astype(BF16), preferred_element_type=F32) * qdec_ref[:, cols[p]])
        kd = (k_ref[:, cols[p]] * kdec_ref[:, cols[p]]).astype(BF16)
        upd = lax.dot_general(kd, vb[p], TN, preferred_element_type=F32)
        s_scr[p] = s_bd * cdec_ref[p] + jnp.where(same_head, upd, 0.0)
    dev = [o[p] - head_mean(o[p]) for p in pairs]
    var = [head_mean(dev[p] * dev[p]) for p in pairs]
    for p in pairs:
        on = dev[p] * lax.rsqrt(var[p] + EPS)
        o_ref[:, cols[p]] = (on * gret_ref[:, cols[p]] * _swish(g_ref[:, cols[p]])).astype(BF16)

    @pl.when(c == pl.num_programs(1) - 1)
    def _():
        for p in range(H_B // 2):
            s_bd = s_scr[p]
            sout_ref[2 * p] = s_bd[0:HEAD_DIM, 0:HEAD_DIM]
            sout_ref[2 * p + 1] = s_bd[HEAD_DIM:LANES, HEAD_DIM:LANES]


def _log_gamma():
    return jnp.log1p(-jnp.exp2(-5.0 - jnp.arange(H_B, dtype=F32)))


def _ret_tables(chunk):
    lg = _log_gamma()
    i = jnp.arange(chunk, dtype=F32)
    diff = i[:, None] - i[None, :]
    dmask = jnp.where(diff[None] >= 0, jnp.exp(jnp.maximum(diff, 0.0)[None] * lg[:, None, None]), 0.0)
    q_dec = jnp.exp((i[:, None] + 1.0) * lg[None, :])
    k_dec = jnp.exp((chunk - 1.0 - i)[:, None] * lg[None, :])
    c_dec = jnp.exp(chunk * lg)
    return dmask, q_dec, k_dec, c_dec


def _ret_prompt_call(qb, kb, vb, gb, g_ret, batch, seq):
    chunk = RET_CHUNK
    n = seq // chunk
    dmask, q_dec, k_dec, c_dec = _ret_tables(chunk)
    qdec = jnp.repeat(q_dec, HEAD_DIM, axis=1)
    kdec = jnp.repeat(k_dec, HEAD_DIM, axis=1)
    cdec = jnp.broadcast_to(jnp.repeat(c_dec, HEAD_DIM).reshape(H_B // 2, LANES, 1),
                            (H_B // 2, LANES, LANES))
    head_of = jnp.arange(LANES) // HEAD_DIM
    avg = jnp.where(head_of[:, None] == head_of[None, :], 1.0 / HEAD_DIM, 0.0).astype(BF16)
    blk = lambda b, c: (b * n + c, 0)
    c2 = lambda b, c: (0, 0)
    c3 = lambda b, c: (0, 0, 0)
    return pl.pallas_call(
        _ret_prompt_kernel, grid=(batch, n),
        in_specs=[pl.BlockSpec((chunk, W_B), blk)] * 4 + [
            pl.BlockSpec((H_B, chunk, chunk), c3),
            pl.BlockSpec((chunk, W_B), c2),
            pl.BlockSpec((chunk, W_B), c2),
            pl.BlockSpec((H_B // 2, LANES, LANES), c3),
            pl.BlockSpec((1, W_B), c2),
            pl.BlockSpec((LANES, LANES), c2)],
        out_specs=(pl.BlockSpec((chunk, W_B), blk),
                   pl.BlockSpec((None, H_B, HEAD_DIM, HEAD_DIM), lambda b, c: (b, 0, 0, 0))),
        out_shape=(jax.ShapeDtypeStruct((batch * seq, W_B), BF16),
                   jax.ShapeDtypeStruct((batch, H_B, HEAD_DIM, HEAD_DIM), F32)),
        scratch_shapes=[pltpu.VMEM((H_B // 2, LANES, LANES), F32)],
        compiler_params=_cparams(("arbitrary", "arbitrary")),
        name="ret_prompt")(qb, kb, vb, gb, dmask, qdec, kdec, cdec, g_ret.reshape(1, W_B), avg)


FF_CHUNK = 1024


def _outffn_kernel(a_ref, b_ref, x_ref, wo_ref, gpost_ref, gfpre_ref, gfpost_ref,
                   wup_ref, wdn_ref, y_ref):
    m = (jnp.dot(a_ref[...], wo_ref[0:W_A, :], preferred_element_type=F32)
         + jnp.dot(b_ref[...], wo_ref[W_A:D_MODEL, :], preferred_element_type=F32))
    x1 = x_ref[...] + _rms(m, gpost_ref[...])
    hn = _rms(x1, gfpre_ref[...]).astype(BF16)
    f = jnp.zeros(x1.shape, F32)
    for c in range(D_FF // FF_CHUNK):
        cs = slice(c * FF_CHUNK, (c + 1) * FF_CHUNK)
        u = jnp.dot(hn, wup_ref[:, cs], preferred_element_type=F32)
        u = jnp.square(jnp.maximum(u, 0.0)).astype(BF16)
        f = f + jnp.dot(u, wdn_ref[cs, :], preferred_element_type=F32)
    y_ref[...] = x1 + _rms(f, gfpost_ref[...])


def _outffn_call(a, b, x, layer, wo, gpost, gfpre, gfpost, wup, wdn, tm):
    m = x.shape[0]
    row = lambda i: (i, 0)
    const = lambda i: (0, 0)
    slab = lambda i: (layer, 0, 0)
    once = dict(pipeline_mode=pl.Buffered(1))
    return pl.pallas_call(
        _outffn_kernel, grid=(m // tm,),
        in_specs=[pl.BlockSpec((tm, W_A), row), pl.BlockSpec((tm, W_B), row),
                  pl.BlockSpec((tm, D_MODEL), row),
                  pl.BlockSpec((None, D_MODEL, D_MODEL), slab, **once),
                  pl.BlockSpec((1, D_MODEL), const), pl.BlockSpec((1, D_MODEL), const),
                  pl.BlockSpec((1, D_MODEL), const),
                  pl.BlockSpec((None, D_MODEL, D_FF), slab, **once),
                  pl.BlockSpec((None, D_FF, D_MODEL), slab, **once)],
        out_specs=pl.BlockSpec((tm, D_MODEL), row),
        out_shape=jax.ShapeDtypeStruct((m, D_MODEL), F32),
        compiler_params=_cparams(("arbitrary",)),
        name="outffn")(a, b, x, wo, gpost.reshape(1, -1), gfpre.reshape(1, -1),
                       gfpost.reshape(1, -1), wup, wdn)


SCORE_SEQS = 8


def _dsa_sample_scores_kernel(pt_ref, *refs, n_pages):
    qi_ref, wi_ref, kinew_ref, o_ref = refs[SCORE_SEQS * n_pages:]
    lane = lax.broadcasted_iota(I32, (1, LANES), 1)
    for j in range(SCORE_SEQS):
        pages = refs[j * n_pages:(j + 1) * n_pages]
        qi = qi_ref[j]
        wi = wi_ref[j]

        def weighted(d):
            r = jnp.sum(wi * jnp.maximum(d, 0.0), axis=0, keepdims=True)
            return jnp.where(r == 0.0, 0.0, r)

        for p in range(n_pages):
            d = jnp.dot(qi, pages[p][...].astype(BF16), preferred_element_type=F32)
            o_ref[j, :, p * LANES:(p + 1) * LANES] = weighted(d)
        knew = kinew_ref[j].astype(BF16).astype(F32)
        d_new = jnp.sum(qi.astype(F32) * knew, axis=1, keepdims=True)
        o_ref[j, :, n_pages * LANES:(n_pages + 1) * LANES] = jnp.where(
            lane == 0, weighted(d_new), -jnp.inf)


def _dsa_sample_scores_call(page_table, pool_ki_t, layer, qi16, wi16, ki_new):
    nb, n_pages = page_table.shape
    ncol = (n_pages + 1) * LANES
    per = lambda b, pt: (b, 0, 0)
    page = lambda j, p: pl.BlockSpec((None, None, D_IDX, PAGE_SIZE),
                                     lambda b, pt: (layer, pt[b * SCORE_SEQS + j, p], 0, 0))
    return pl.pallas_call(
        functools.partial(_dsa_sample_scores_kernel, n_pages=n_pages),
        grid_spec=pltpu.PrefetchScalarGridSpec(
            num_scalar_prefetch=1, grid=(nb // SCORE_SEQS,),
            in_specs=[page(j, p) for j in range(SCORE_SEQS) for p in range(n_pages)] + [
                pl.BlockSpec((SCORE_SEQS, 16, D_IDX), per),
                pl.BlockSpec((SCORE_SEQS, 16, 1), per),
                pl.BlockSpec((SCORE_SEQS, 1, D_IDX), per)],
            out_specs=pl.BlockSpec((SCORE_SEQS, 1, ncol), per)),
        out_shape=jax.ShapeDtypeStruct((nb, 1, ncol), F32),
        compiler_params=_cparams(("arbitrary",)),
        name="dsa_sample_scores")(page_table, *([pool_ki_t] * (SCORE_SEQS * n_pages)),
                                  qi16, wi16, ki_new)


def _dsa_sample_select_kernel(sc_ref, sel_ref, sc_scr, pos_scr, *, topk, n_valid):
    ncol = sc_ref.shape[1]
    for p in range(ncol // LANES):
        rows = slice(p * LANES, (p + 1) * LANES)
        sc_scr[0, rows, :] = sc_ref[:, rows].T
    thr, cut = _topk_threshold(sc_scr, pos_scr, 1, ncol, topk)
    krow = lax.broadcasted_iota(I32, (LANES, LANES), 0)
    for p in range(ncol // LANES):
        rows = slice(p * LANES, (p + 1) * LANES)
        k = sc_scr[0, rows, :]
        kpos = krow + p * LANES
        sel = ((k > thr) | ((k == thr) & (kpos < cut))) & (kpos < n_valid)
        sel_ref[:, rows] = jnp.where(sel, 1.0, 0.0).T


def _dsa_sample_select_call(sc, topk, n_valid):
    nb, ncol = sc.shape
    return pl.pallas_call(
        functools.partial(_dsa_sample_select_kernel, topk=topk, n_valid=n_valid),
        out_shape=jax.ShapeDtypeStruct((nb, ncol), F32),
        scratch_shapes=[pltpu.VMEM((1, ncol, nb), F32), pltpu.VMEM((1, ncol, nb), F32)],
        compiler_params=pltpu.CompilerParams(vmem_limit_bytes=VMEM_LIMIT),
        name="dsa_sample_select")(sc)


def _dsa_sample_attn_kernel(pt_ref, *refs, n_pages):
    kpages = refs[:n_pages]
    vpages = refs[n_pages:2 * n_pages]
    q_ref, knew_ref, vnew_ref, sel_ref, o_ref = refs[2 * n_pages:]
    rows = lax.broadcasted_iota(I32, (16, W_A), 0)
    lanes = lax.broadcasted_iota(I32, (16, W_A), 1)
    own_head = lax.shift_right_logical(lanes, 6) == rows
    q_bd = jnp.where(own_head, jnp.broadcast_to(q_ref[...], (16, W_A)), 0.0).astype(BF16)
    first_pos = lax.broadcasted_iota(I32, (PAGE_SIZE, W_A), 0) == 0

    def new_tile(ref):
        return jnp.where(first_pos, jnp.broadcast_to(ref[...], (PAGE_SIZE, W_A)), 0.0).astype(BF16)

    def page_t(ref):
        return ref[...].reshape(W_A, PAGE_SIZE).astype(BF16)

    s_all = []
    for p in range(n_pages + 1):
        if p < n_pages:
            s = jnp.dot(q_bd, page_t(kpages[p]), preferred_element_type=F32)
        else:
            s = lax.dot_general(q_bd, new_tile(knew_ref), NT, preferred_element_type=F32)
        s_all.append(jnp.where(sel_ref[:, p * LANES:(p + 1) * LANES] > 0.5, s, NEG))
    m = s_all[0]
    for s in s_all[1:]:
        m = jnp.maximum(m, s)
    m = jnp.max(m, axis=1, keepdims=True)
    p_all = [jnp.exp2(s - m) for s in s_all]
    l = p_all[0]
    for pp in p_all[1:]:
        l = l + pp
    inv = 1.0 / jnp.sum(l, axis=1, keepdims=True)
    acc = jnp.zeros((16, W_A), F32)
    for p in range(n_pages + 1):
        pn = (p_all[p] * inv).astype(BF16)
        if p < n_pages:
            acc = acc + lax.dot_general(pn, page_t(vpages[p]), NT, preferred_element_type=F32)
        else:
            acc = acc + jnp.dot(pn, new_tile(vnew_ref), preferred_element_type=F32)
    o_ref[...] = jnp.sum(jnp.where(own_head, acc, 0.0), axis=0, keepdims=True)


def _dsa_sample_attn_call(page_table, pool_k_t, pool_v_t, layer, q, k_new, v_new, sel):
    nb, n_pages = page_table.shape
    per = lambda b, pt: (b, 0, 0)
    page = lambda p: pl.BlockSpec((None, None, H_A, HEAD_DIM, PAGE_SIZE),
                                  lambda b, pt: (layer, pt[b, p], 0, 0, 0))
    row_blk = pl.BlockSpec((None, 1, W_A), per)
    return pl.pallas_call(
        functools.partial(_dsa_sample_attn_kernel, n_pages=n_pages),
        grid_spec=pltpu.PrefetchScalarGridSpec(
            num_scalar_prefetch=1, grid=(nb,),
            in_specs=[page(p) for p in range(n_pages)] * 2 + [
                row_blk, row_blk, row_blk,
                pl.BlockSpec((None, 1, sel.shape[2]), per)],
            out_specs=row_blk),
        out_shape=jax.ShapeDtypeStruct((nb, 1, W_A), F32),
        compiler_params=_cparams(("arbitrary",)),
        name="dsa_sample_attn")(page_table, *([pool_k_t] * n_pages), *([pool_v_t] * n_pages),
                                q, k_new, v_new, sel)


def _ret_sample_kernel(q_ref, k_ref, v_ref, g_ref, s_ref, gret_ref, gam_ref, o_ref, sout_ref):
    v = v_ref[...]
    gam = gam_ref[...]

    def body(d, o):
        s_new = s_ref[d] * gam + k_ref[pl.ds(d, 1), :] * v
        sout_ref[d] = s_new
        return o + q_ref[pl.ds(d, 1), :] * s_new

    o = lax.fori_loop(0, HEAD_DIM, body, jnp.zeros(v.shape, F32))
    mu = jnp.mean(o, axis=0, keepdims=True)
    var = jnp.mean(jnp.square(o - mu), axis=0, keepdims=True)
    on = (o - mu) * lax.rsqrt(var + EPS)
    o_ref[...] = on * gret_ref[...] * _swish(g_ref[...])


def _ret_sample_call(qb, kb, vb, gb, state_t, layer, g_ret):
    nb = qb.shape[0]
    gam = jnp.exp(1.0 * _log_gamma())
    gam_b = jnp.broadcast_to(jnp.repeat(gam, HEAD_DIM)[:, None], (W_B, nb))
    gret_b = jnp.broadcast_to(g_ret[:, None], (W_B, nb))
    head = pl.BlockSpec((HEAD_DIM, nb), lambda h: (h, 0))
    o_t, s_new = pl.pallas_call(
        _ret_sample_kernel, grid=(H_B,),
        in_specs=[head, head, head, head,
                  pl.BlockSpec((None, None, HEAD_DIM, HEAD_DIM, nb), lambda h: (layer, h, 0, 0, 0)),
                  head, head],
        out_specs=(head, pl.BlockSpec((None, HEAD_DIM, HEAD_DIM, nb), lambda h: (h, 0, 0, 0))),
        out_shape=(jax.ShapeDtypeStruct((W_B, nb), F32),
                   jax.ShapeDtypeStruct((H_B, HEAD_DIM, HEAD_DIM, nb), F32)),
        compiler_params=_cparams(("arbitrary",)),
        name="ret_sample")(qb.T, kb.T, vb.T, gb.T, state_t, gret_b, gam_b)
    return o_t.T.astype(BF16), s_new


def _split_w_in_t(w_t):
    d = w_t.shape[1]
    starts = np.cumsum((0,) + COL_SIZES).tolist()
    qi = w_t[starts[3]:starts[4]]
    kiwi = w_t[starts[4]:starts[6]]
    qi_pad = jnp.pad(qi.reshape(H_IDX, D_IDX, d), ((0, 0), (0, LANES - D_IDX), (0, 0))).reshape(-1, d)
    w_mid = jnp.concatenate([qi_pad, kiwi, jnp.zeros((LANES - D_IDX - H_IDX, d), w_t.dtype)], axis=0)
    return w_mid, w_t[starts[6]:]


def _rope_tables(pos):
    half = HEAD_DIM // 2
    inv = ROPE_THETA ** (-jnp.arange(half, dtype=F32) / half)
    ang = pos[:, None] * inv[None, :]
    c = jnp.cos(ang)
    s = jnp.sin(ang)
    return jnp.concatenate([c, c, c, c], axis=1), jnp.concatenate([-s, s, -s, s], axis=1)


def kernel(x_prompt, x_sample, cache_k, cache_v, cache_kidx, state_ret, page_table, w_in, w_out,
           g_ret, g_mix_pre, g_mix_post, g_ffn_pre, g_ffn_post, w_up, w_down):
    batch, seq, _ = x_prompt.shape
    nb, dec_seq, _ = x_sample.shape
    depth = w_in.shape[0]
    n_pages = page_table.shape[1]
    past = n_pages * PAGE_SIZE
    assert dec_seq == 1 and seq % DSA_CH == 0 and nb == LANES
    tm_p = DSA_CH

    cos_p, sin_p = _rope_tables(jnp.arange(seq, dtype=F32))
    cos_s, sin_s = _rope_tables(past + jnp.arange(dec_seq, dtype=F32))
    cos_s = jnp.broadcast_to(cos_s, (nb, LANES))
    sin_s = jnp.broadcast_to(sin_s, (nb, LANES))
    pool_k_t = jnp.transpose(cache_k, (0, 1, 3, 4, 2))
    pool_v_t = jnp.transpose(cache_v, (0, 1, 3, 4, 2))
    pool_ki_t = jnp.swapaxes(cache_kidx, 2, 3)
    state_t = jnp.transpose(state_ret, (0, 2, 3, 4, 1))
    w_in_t = jnp.transpose(w_in, (2, 0, 1))
    wo_bf = w_out.astype(BF16)
    wup_bf = w_up.astype(BF16)
    wdn_bf = w_down.astype(BF16)
    topk_s = min(TOPK_MAX, (past + dec_seq) // 4)

    yp = x_prompt.reshape(batch * seq, D_MODEL)
    ys = x_sample.reshape(nb, D_MODEL)
    kp, vp, kip, sp, kss, vss, kis, ss = [], [], [], [], [], [], [], []
    for l in range(depth):
        w_t = w_in_t[:, l, :].astype(BF16)
        w_mid, w_tail = _split_w_in_t(w_t)
        gpre = g_mix_pre[l].reshape(1, D_MODEL)

        (qa, ka, ka_bf, va, va_t, qi, kiwi, kiwi_bf, ki, qb, kb, vb, gb) = _proj_call(
            yp, gpre, w_t, w_mid, w_tail, cos_p, sin_p, tm_p)
        a = _dsa_prompt_call(qa, qi, kiwi, ka_bf, va_t, kiwi_bf, batch, seq)
        bmix, s_new = _ret_prompt_call(qb, kb, vb, gb, g_ret[l], batch, seq)
        yp = _outffn_call(a, bmix, yp, l, wo_bf, g_mix_post[l], g_ffn_pre[l], g_ffn_post[l],
                          wup_bf, wdn_bf, 512)
        kp.append(ka); vp.append(va); kip.append(ki); sp.append(s_new)

        (qa, ka, ka_bf, va, va_t, qi, kiwi, kiwi_bf, ki, qb, kb, vb, gb) = _proj_call(
            ys, gpre, w_t, w_mid, w_tail, cos_s, sin_s, nb)
        qi16 = jnp.pad(qi.reshape(nb, H_IDX, LANES)[:, :, :D_IDX], ((0, 0), (0, 16 - H_IDX), (0, 0)))
        wi16 = jnp.pad(kiwi[:, D_IDX:D_IDX + H_IDX], ((0, 0), (0, 16 - H_IDX)))[:, :, None]
        sc = _dsa_sample_scores_call(page_table, pool_ki_t, l, qi16, wi16, ki.reshape(nb, 1, D_IDX))
        sel = _dsa_sample_select_call(sc.reshape(nb, -1), topk_s, past + dec_seq)
        a = _dsa_sample_attn_call(page_table, pool_k_t, pool_v_t, l,
                                  qa.astype(F32).reshape(nb, 1, W_A),
                                  ka.reshape(nb, 1, W_A), va.reshape(nb, 1, W_A),
                                  sel.reshape(nb, 1, -1))
        bmix, s_new = _ret_sample_call(qb, kb, vb, gb, state_t, l, g_ret[l])
        ys = _outffn_call(a.reshape(nb, W_A).astype(BF16), bmix, ys, l, wo_bf, g_mix_post[l],
                          g_ffn_pre[l], g_ffn_post[l], wup_bf, wdn_bf, nb)
        kss.append(ka); vss.append(va); kis.append(ki); ss.append(s_new)

    def heads(xs, lead):
        return jnp.stack(xs).reshape((depth,) + lead + (H_A, HEAD_DIM))

    ret_s = jnp.transpose(jnp.stack(ss), (0, 4, 1, 2, 3))
    return (yp.reshape(batch, seq, D_MODEL), ys.reshape(nb, dec_seq, D_MODEL),
            heads(kp, (batch, seq)), heads(vp, (batch, seq)),
            jnp.stack(kip).reshape(depth, batch, seq, D_IDX), jnp.stack(sp),
            heads(kss, (nb, dec_seq)), heads(vss, (nb, dec_seq)),
            jnp.stack(kis).reshape(depth, nb, dec_seq, D_IDX), ret_s)
```

```python
import functools

import numpy as np
import jax
import jax.numpy as jnp
from jax import lax
from jax.experimental import pallas as pl
from jax.experimental.pallas import tpu as pltpu

F32 = jnp.float32
BF16 = jnp.bfloat16
I32 = jnp.int32

D_MODEL = 1024
HEAD_DIM = 64
H_A = 8
H_B = 8
H_IDX = 4
D_IDX = 64
W_A = H_A * HEAD_DIM
W_B = H_B * HEAD_DIM
D_FF = 4 * D_MODEL
TOPK_MAX = 256
PAGE_SIZE = 128
ROPE_THETA = 10000.0
EPS = 1e-6
Q_BLOCK = 128
RET_CHUNK = 128
COL_SIZES = (W_A, W_A, W_A, H_IDX * D_IDX, D_IDX, H_IDX, W_B, W_B, W_B, W_B)

LANES = 128
NEG = -1e30
VMEM_LIMIT = 48 * 1024 * 1024

LOG2E = 1.4426950408889634
VT_ROWS = 144
VT_ALL = (H_A // 2) * VT_ROWS

NT = (((1,), (1,)), ((), ()))
TN = (((0,), (0,)), ((), ()))

C_QA, C_KA, C_VA, C_QI, C_KIWI, C_QB, C_KB, C_VB, C_GB = (
    0, 512, 1024, 1536, 2048, 2176, 2688, 3200, 3712)
N_PACK = 4224


def _cparams(sem):
    return pltpu.CompilerParams(dimension_semantics=sem, vmem_limit_bytes=VMEM_LIMIT)


def _rms(x, g):
    return x * lax.rsqrt(jnp.mean(x * x, axis=-1, keepdims=True) + EPS) * g


def _proj_kernel(x_ref, g_ref, whead_ref, wmid_ref, wtail_ref, cos_ref, sin_ref,
                 qa_ref, ka_ref, kabf_ref, va_ref, vat_ref, qi_ref, kiwi_ref, kiwibf_ref,
                 ki_ref, qb_ref, kb_ref, vb_ref, gb_ref):
    x = x_ref[...]
    h = _rms(x, g_ref[...]).astype(BF16)
    cos = cos_ref[...]
    sin = sin_ref[...]
    lane = lax.broadcasted_iota(I32, cos.shape, 1)
    first_half = (lane % 64) < 32

    def rope(z, c, s):
        sw = jnp.where(first_half, pltpu.roll(z, 96, 1), pltpu.roll(z, 32, 1))
        return z * c + sw * s

    def proj(c0, n):
        if c0 < C_QI:
            w = whead_ref[c0:c0 + n, :]
        elif c0 < C_QB:
            w = wmid_ref[c0 - C_QI:c0 - C_QI + n, :]
        else:
            w = wtail_ref[c0 - C_QB:c0 - C_QB + n, :]
        return lax.dot_general(h, w, NT, preferred_element_type=F32)

    def roped(c0, scale):
        z = proj(c0, 512)
        parts = []
        for j in range(4):
            r = rope(z[:, j * LANES:(j + 1) * LANES], cos, sin)
            parts.append(r if scale is None else r * scale)
        return parts

    qa = roped(C_QA, HEAD_DIM ** -0.5 * LOG2E)
    for j in range(4):
        qa_ref[:, j * LANES:(j + 1) * LANES] = qa[j].astype(BF16)
    ka = roped(C_KA, None)
    for j in range(4):
        ka_ref[:, j * LANES:(j + 1) * LANES] = ka[j]
        kabf_ref[:, j * LANES:(j + 1) * LANES] = ka[j].astype(BF16)
    va = proj(C_VA, 512)
    va_ref[...] = va
    va_t = va.T
    ones = jnp.ones((VT_ROWS - LANES, va_t.shape[1]), BF16)
    for p in range(H_A // 2):
        vat_ref[p * VT_ROWS:p * VT_ROWS + LANES, :] = va_t[p * LANES:(p + 1) * LANES, :].astype(BF16)
        vat_ref[p * VT_ROWS + LANES:(p + 1) * VT_ROWS, :] = ones
    qi = roped(C_QI, None)
    for j in range(4):
        qi_ref[:, j * LANES:(j + 1) * LANES] = qi[j].astype(BF16)
    zk = proj(C_KIWI, LANES)
    is_key = lane < 64
    kiwi = rope(zk, jnp.where(is_key, cos, (H_IDX ** -0.5) * (D_IDX ** -0.5)),
                jnp.where(is_key, sin, 0.0))
    kiwi_ref[...] = kiwi
    kiwibf_ref[...] = kiwi.astype(BF16)
    ki_ref[...] = kiwi[:, 0:64]
    qb = roped(C_QB, None)
    for j in range(4):
        qb_ref[:, j * LANES:(j + 1) * LANES] = qb[j]
    kb = roped(C_KB, HEAD_DIM ** -0.5)
    for j in range(4):
        kb_ref[:, j * LANES:(j + 1) * LANES] = kb[j]
    vb_ref[...] = proj(C_VB, 512)
    gb_ref[...] = proj(C_GB, 512)


def _proj_call(x, g, w_t, w_mid, w_tail, cos_t, sin_t, tm):
    m = x.shape[0]
    once = dict(pipeline_mode=pl.Buffered(1))
    n_t = cos_t.shape[0] // tm
    row = lambda i: (i, 0)
    const = lambda i: (0, 0)
    tab = lambda i: (i % n_t, 0)
    wide = lambda dt: jax.ShapeDtypeStruct((m, 512), dt)
    out_shape = (wide(BF16), wide(F32), wide(BF16), wide(F32),
                 jax.ShapeDtypeStruct((m // tm, VT_ALL, tm), BF16), wide(BF16),
                 jax.ShapeDtypeStruct((m, LANES), F32), jax.ShapeDtypeStruct((m, LANES), BF16),
                 jax.ShapeDtypeStruct((m, 64), F32), wide(F32), wide(F32), wide(F32), wide(F32))
    out_specs = tuple(
        pl.BlockSpec((None, VT_ALL, tm), lambda i: (i, 0, 0)) if len(s.shape) == 3
        else pl.BlockSpec((tm, s.shape[1]), row) for s in out_shape)
    return pl.pallas_call(
        _proj_kernel, grid=(m // tm,),
        in_specs=[pl.BlockSpec((tm, D_MODEL), row),
                  pl.BlockSpec((1, D_MODEL), const),
                  pl.BlockSpec((C_QI, D_MODEL), const, **once),
                  pl.BlockSpec((C_QB - C_QI, D_MODEL), const, **once),
                  pl.BlockSpec((N_PACK - C_QB, D_MODEL), const, **once),
                  pl.BlockSpec((tm, LANES), tab),
                  pl.BlockSpec((tm, LANES), tab)],
        out_specs=out_specs, out_shape=out_shape,
        compiler_params=_cparams(("arbitrary",)), name="proj")(x, g, w_t, w_mid, w_tail, cos_t, sin_t)


COUNT_ROWS = 32
KEY_NEG_INF = -2139095041
RANK_TOP = 65535
BIG = 3e38


def _key_to_float(key):
    bits = key ^ ((key >> 31) & 0x7FFFFFFF)
    return jnp.where(key <= KEY_NEG_INF, -jnp.inf, pltpu.bitcast(bits, F32))


def _count(scr, nc, ch, pred):
    def body(c, acc):
        m = jnp.where(pred(scr[c], c), 1.0, 0.0)
        for r in range(ch // COUNT_ROWS):
            acc = acc + m[r * COUNT_ROWS:(r + 1) * COUNT_ROWS]
        return acc

    acc = lax.fori_loop(0, nc, body, jnp.zeros((COUNT_ROWS, LANES), F32))
    return jnp.sum(acc, axis=0, keepdims=True)


def _bit_search(scr, nc, ch, target, init, nbits, cand_float):
    def bit_body(j, carry):
        lo, lo_cnt, up_cnt = carry
        cand = lo + jnp.left_shift(jnp.int32(1), nbits - 1 - j)
        cand_f = cand_float(cand)
        cnt = _count(scr, nc, ch, lambda s, c: s >= cand_f)
        ok = cnt >= target
        return jnp.where(ok, cand, lo), jnp.where(ok, cnt, lo_cnt), jnp.where(ok, up_cnt, cnt)

    return lax.fori_loop(0, nbits, bit_body, init)


def _topk_threshold(sc_scr, pos_scr, nc, ch, topk):
    assert ch >= topk and ch % COUNT_ROWS == 0
    kf = float(topk)
    total = (nc * ch).astype(F32) if hasattr(nc, "astype") else float(nc * ch)
    cnt0 = _count(sc_scr, nc, ch, lambda s, c: s >= 0.0)
    ok0 = cnt0 >= kf
    init = (jnp.where(ok0, 0, -32768), jnp.where(ok0, cnt0, total), jnp.where(ok0, 0.0, cnt0))
    t_hi, at_hi, above_hi = _bit_search(sc_scr, nc, ch, kf, init, 15,
                                        lambda v: _key_to_float(v << 16))
    need_lo = kf - above_hi
    bucket_n = at_hi - above_hi
    bucket_lo = _key_to_float(t_hi << 16)
    bucket_up = _key_to_float((t_hi + 1) << 16)

    def top_body(c, top):
        s = sc_scr[c]
        s_b = jnp.where((s >= bucket_lo) & jnp.logical_not(s >= bucket_up), s, -jnp.inf)
        for r in range(ch // COUNT_ROWS):
            top = jnp.maximum(top, s_b[r * COUNT_ROWS:(r + 1) * COUNT_ROWS])
        return top

    top = lax.fori_loop(0, nc, top_body, jnp.full((COUNT_ROWS, LANES), -jnp.inf, F32))
    v_top = jnp.max(top, axis=0, keepdims=True)
    n_top = _count(sc_scr, nc, ch, lambda s, c: s == v_top)
    one_key = (n_top == bucket_n) & (bucket_n > need_lo)
    krow = lax.broadcasted_iota(I32, (ch, LANES), 0)

    @pl.when(jnp.max(jnp.where(one_key, 1.0, 0.0)) > 0.0)
    def _():
        def rank_body(c, carry):
            s = sc_scr[c]
            rank = (RANK_TOP - (krow + c * ch)).astype(F32)
            ranked = jnp.where(s > v_top, BIG, jnp.where(s == v_top, rank, -BIG))
            sc_scr[c] = jnp.where(one_key, ranked, s)
            return carry

        lax.fori_loop(0, nc, rank_body, 0)

    def second_level(v):
        return jnp.where(one_key, v.astype(F32), _key_to_float((t_hi << 16) | v))

    t_lo, at_lo, above_lo = _bit_search(sc_scr, nc, ch, kf,
                                        (jnp.zeros_like(t_hi), at_hi, above_hi), 16, second_level)
    thr = second_level(t_lo)
    need_tie = kf - above_lo

    n_tied = at_lo - above_lo

    def tie_search():
        def tie_prep(c, carry):
            pos_scr[c] = jnp.where(sc_scr[c] == thr, (krow + c * ch).astype(F32), BIG)
            return carry

        lax.fori_loop(0, nc, tie_prep, 0)

        def tie_body(j, cut):
            cand = cut + jnp.left_shift(jnp.int32(1), 12 - j)
            cand_f = cand.astype(F32)
            f = _count(pos_scr, nc, ch, lambda p, c: p < cand_f)
            return jnp.where(f <= need_tie, cand, cut)

        return lax.fori_loop(0, 13, tie_body, jnp.zeros((1, LANES), I32))

    cut = lax.cond(jnp.max(n_tied - need_tie) > 0.0, tie_search,
                   lambda: jnp.full((1, LANES), 1 << 30, I32))
    return thr, cut


DSA_CH = 512


def _dsa_prompt_kernel(qa_ref, qi_ref, kiwi_ref, ka_ref, vat_ref, kibf_ref, o_ref,
                       sc_scr, pos_scr, bias_scr, acc_scr, qcat_scr, qicat_scr,
                       sa_scr, sb_scr, *, topk):
    ch = DSA_CH
    n_pair = H_A // 2
    i = pl.program_id(1)
    t0 = i * Q_BLOCK
    nc = lax.div(t0 + Q_BLOCK + ch - 1, ch)
    qpos = t0 + lax.broadcasted_iota(I32, (ch, LANES), 1)
    krow = lax.broadcasted_iota(I32, (ch, LANES), 0)
    wi_t = kiwi_ref[...].T

    d_lo = lax.broadcasted_iota(I32, (LANES, Q_BLOCK), 0) < HEAD_DIM
    for p in range(n_pair):
        qt = qa_ref[:, p * LANES:(p + 1) * LANES].astype(F32).T
        qcat_scr[p, :, 0:Q_BLOCK] = jnp.where(d_lo, qt, 0.0).astype(BF16)
        qcat_scr[p, :, Q_BLOCK:2 * Q_BLOCK] = jnp.where(d_lo, 0.0, qt).astype(BF16)
    for g in range(H_IDX // 2):
        for e in range(2):
            h = 2 * g + e
            qicat_scr[g, :, e * Q_BLOCK:(e + 1) * Q_BLOCK] = (
                qi_ref[:, h * LANES:(h + 1) * LANES].astype(F32).T.astype(BF16))

    def score_body(c, carry):
        off = pl.multiple_of(c * ch, ch)
        kic = kibf_ref[pl.ds(off, ch), :]
        acc = jnp.zeros((ch, LANES), F32)
        for g in range(H_IDX // 2):
            d = jnp.dot(kic, qicat_scr[g], preferred_element_type=F32)
            acc = (acc + wi_t[64 + 2 * g:65 + 2 * g, :] * jnp.maximum(d[:, 0:LANES], 0.0)
                   + wi_t[65 + 2 * g:66 + 2 * g, :] * jnp.maximum(d[:, LANES:2 * LANES], 0.0))
        acc = jnp.where(acc == 0.0, 0.0, acc)
        sc_scr[c] = jnp.where(krow + c * ch <= qpos, acc, -jnp.inf)
        return carry

    lax.fori_loop(0, nc, score_body, 0)
    thr, cut = _topk_threshold(sc_scr, pos_scr, nc, ch, topk)
    odd = (nc & 1) == 1

    def bias_body(c, carry):
        k = sc_scr[c]
        kpos = krow + c * ch
        sel = ((k > thr) | ((k == thr) & (kpos < cut))) & (kpos <= qpos)
        bias_scr[c] = jnp.where(sel, 0.0, NEG)
        return carry

    lax.fori_loop(0, nc, bias_body, 0)

    row_lo = lax.broadcasted_iota(I32, (LANES, Q_BLOCK), 0) < HEAD_DIM
    acc_scr[...] = jnp.zeros_like(acc_scr)
    last_chunk = ka_ref.shape[0] // ch - 1

    def qk_scores(c, p):
        off = pl.multiple_of(jnp.minimum(c, last_chunk) * ch, ch)
        return jnp.dot(ka_ref[pl.ds(off, ch), p * LANES:(p + 1) * LANES], qcat_scr[p],
                       preferred_element_type=F32)

    for p in range(n_pair):
        sa_scr[p] = qk_scores(0, p)

    def half_step(c, cur_scr, nxt_scr, carry, prefetch=True):
        b = bias_scr[c]
        new = []
        if prefetch:
            for p in range(n_pair):
                nxt_scr[p] = qk_scores(c + 1, p)
        for p in range(n_pair):
            s = cur_scr[p]
            alphas, pts, ms = [], [], []
            for e in range(2):
                m = carry[4 * p + 2 * e]
                sb = s[:, e * LANES:(e + 1) * LANES] + b
                m_new = jnp.maximum(m, jnp.max(sb, axis=0, keepdims=True))
                alphas.append(jnp.exp2(m - m_new))
                pts.append(jnp.exp2(sb - m_new).astype(BF16))
                ms.append(m_new)
            pv = jnp.dot(vat_ref[c, p * VT_ROWS:(p + 1) * VT_ROWS, :],
                         jnp.concatenate(pts, axis=1), preferred_element_type=F32)
            for e in range(2):
                l = carry[4 * p + 2 * e + 1]
                new += [ms[e], alphas[e] * l + pv[LANES:LANES + 1, e * LANES:(e + 1) * LANES]]
            acc_scr[p] = (acc_scr[p] * jnp.where(row_lo, alphas[0], alphas[1])
                          + jnp.where(row_lo, pv[0:LANES, 0:LANES], pv[0:LANES, LANES:2 * LANES]))
        return tuple(new)

    def att_body(j, carry):
        carry = half_step(2 * j, sa_scr, sb_scr, carry)
        return half_step(2 * j + 1, sb_scr, sa_scr, carry)

    init = (jnp.full((1, LANES), NEG, F32), jnp.zeros((1, LANES), F32)) * H_A
    stats = lax.fori_loop(0, lax.shift_right_logical(nc, 1), att_body, init)
    stats = lax.cond(odd, lambda st: half_step(nc - 1, sa_scr, sb_scr, st, prefetch=False),
                     lambda st: st, stats)
    for p in range(n_pair):
        l_pair = jnp.where(row_lo, stats[4 * p + 1], stats[4 * p + 3])
        o_ref[:, p * LANES:(p + 1) * LANES] = (acc_scr[p] / l_pair).T.astype(BF16)


def _dsa_prompt_call(qa, qi, kiwi, ka_bf, va_t, kiwi_bf, batch, seq):
    nq = seq // Q_BLOCK
    topk = min(TOPK_MAX, seq // 4)
    nch = seq // DSA_CH
    blk = lambda b, i: (b * nq + i, 0)
    full = lambda b, i: (b, 0)
    return pl.pallas_call(
        functools.partial(_dsa_prompt_kernel, topk=topk), grid=(batch, nq),
        in_specs=[pl.BlockSpec((Q_BLOCK, W_A), blk),
                  pl.BlockSpec((Q_BLOCK, 512), blk),
                  pl.BlockSpec((Q_BLOCK, LANES), blk),
                  pl.BlockSpec((seq, W_A), full),
                  pl.BlockSpec((nch, VT_ALL, DSA_CH), lambda b, i: (b, 0, 0)),
                  pl.BlockSpec((seq, LANES), full)],
        out_specs=pl.BlockSpec((Q_BLOCK, W_A), blk),
        out_shape=jax.ShapeDtypeStruct((batch * seq, W_A), BF16),
        scratch_shapes=[pltpu.VMEM((nch, DSA_CH, LANES), F32),
                        pltpu.VMEM((nch, DSA_CH, LANES), F32),
                        pltpu.VMEM((nch, DSA_CH, LANES), F32),
                        pltpu.VMEM((H_A // 2, LANES, Q_BLOCK), F32),
                        pltpu.VMEM((H_A // 2, LANES, 2 * Q_BLOCK), BF16),
                        pltpu.VMEM((H_IDX // 2, LANES, 2 * Q_BLOCK), BF16),
                        pltpu.VMEM((H_A // 2, DSA_CH, 2 * Q_BLOCK), F32),
                        pltpu.VMEM((H_A // 2, DSA_CH, 2 * Q_BLOCK), F32)],
        compiler_params=_cparams(("arbitrary", "arbitrary")),
        name="dsa_prompt")(qa, qi, kiwi, ka_bf, va_t, kiwi_bf)


def _swish(g):
    return g * (1.0 / (1.0 + jnp.exp(-g)))


def _ret_prompt_kernel(q_ref, k_ref, v_ref, g_ref, dmask_ref, qdec_ref, kdec_ref, cdec_ref,
                       gret_ref, avg_ref, o_ref, sout_ref, s_scr):
    c = pl.program_id(1)

    @pl.when(c == 0)
    def _():
        s_scr[...] = jnp.zeros_like(s_scr)

    lane_lo = lax.broadcasted_iota(I32, (RET_CHUNK, LANES), 1) < HEAD_DIM
    avg = avg_ref[...]
    same_head = (lax.broadcasted_iota(I32, (LANES, LANES), 0) < HEAD_DIM) == (
        lax.broadcasted_iota(I32, (LANES, LANES), 1) < HEAD_DIM)
    pairs = range(H_B // 2)
    cols = [slice(p * LANES, (p + 1) * LANES) for p in pairs]

    def head_mean(x):
        x_hi = x.astype(BF16)
        x_lo = (x - x_hi.astype(F32)).astype(BF16)
        return (jnp.dot(x_hi, avg, preferred_element_type=F32)
                + jnp.dot(x_lo, avg, preferred_element_type=F32))

    qb = [q_ref[:, cols[p]].astype(BF16) for p in pairs]
    kb = [k_ref[:, cols[p]].astype(BF16) for p in pairs]
    vb = [v_ref[:, cols[p]].astype(BF16) for p in pairs]
    att = []
    for p in pairs:
        zero = jnp.zeros_like(qb[p])
        for e, qe in enumerate((jnp.where(lane_lo, qb[p], zero), jnp.where(lane_lo, zero, qb[p]))):
            att.append(lax.dot_general(qe, kb[p], NT, preferred_element_type=F32) * dmask_ref[2 * p + e])
    o = []
    for p in pairs:
        intra = [jnp.dot(att[2 * p + e].astype(BF16), vb[p], preferred_element_type=F32)
                 for e in range(2)]
        s_bd = s_scr[p]
        o.append(jnp.where(lane_lo, intra[0], intra[1])
                 + jnp.dot(qb[p], s_bd.astype(BF16), preferred_element_type=F32) * qdec_ref[:, cols[p]])
        kd = (k_ref[:, cols[p]] * kdec_ref[:, cols[p]]).astype(BF16)
        upd = lax.dot_general(kd, vb[p], TN, preferred_element_type=F32)
        s_scr[p] = s_bd * cdec_ref[p] + jnp.where(same_head, upd, 0.0)
    dev = [o[p] - head_mean(o[p]) for p in pairs]
    var = [head_mean(dev[p] * dev[p]) for p in pairs]
    for p in pairs:
        on = dev[p] * lax.rsqrt(var[p] + EPS)
        o_ref[:, cols[p]] = (on * gret_ref[:, cols[p]] * _swish(g_ref[:, cols[p]])).astype(BF16)

    @pl.when(c == pl.num_programs(1) - 1)
    def _():
        for p in range(H_B // 2):
            s_bd = s_scr[p]
            sout_ref[2 * p] = s_bd[0:HEAD_DIM, 0:HEAD_DIM]
            sout_ref[2 * p + 1] = s_bd[HEAD_DIM:LANES, HEAD_DIM:LANES]


def _log_gamma():
    return jnp.log1p(-jnp.exp2(-5.0 - jnp.arange(H_B, dtype=F32)))


def _ret_tables(chunk):
    lg = _log_gamma()
    i = jnp.arange(chunk, dtype=F32)
    diff = i[:, None] - i[None, :]
    dmask = jnp.where(diff[None] >= 0, jnp.exp(jnp.maximum(diff, 0.0)[None] * lg[:, None, None]), 0.0)
    q_dec = jnp.exp((i[:, None] + 1.0) * lg[None, :])
    k_dec = jnp.exp((chunk - 1.0 - i)[:, None] * lg[None, :])
    c_dec = jnp.exp(chunk * lg)
    return dmask, q_dec, k_dec, c_dec


def _ret_prompt_call(qb, kb, vb, gb, g_ret, batch, seq):
    chunk = RET_CHUNK
    n = seq // chunk
    dmask, q_dec, k_dec, c_dec = _ret_tables(chunk)
    qdec = jnp.repeat(q_dec, HEAD_DIM, axis=1)
    kdec = jnp.repeat(k_dec, HEAD_DIM, axis=1)
    cdec = jnp.broadcast_to(jnp.repeat(c_dec, HEAD_DIM).reshape(H_B // 2, LANES, 1),
                            (H_B // 2, LANES, LANES))
    head_of = jnp.arange(LANES) // HEAD_DIM
    avg = jnp.where(head_of[:, None] == head_of[None, :], 1.0 / HEAD_DIM, 0.0).astype(BF16)
    blk = lambda b, c: (b * n + c, 0)
    c2 = lambda b, c: (0, 0)
    c3 = lambda b, c: (0, 0, 0)
    return pl.pallas_call(
        _ret_prompt_kernel, grid=(batch, n),
        in_specs=[pl.BlockSpec((chunk, W_B), blk)] * 4 + [
            pl.BlockSpec((H_B, chunk, chunk), c3),
            pl.BlockSpec((chunk, W_B), c2),
            pl.BlockSpec((chunk, W_B), c2),
            pl.BlockSpec((H_B // 2, LANES, LANES), c3),
            pl.BlockSpec((1, W_B), c2),
            pl.BlockSpec((LANES, LANES), c2)],
        out_specs=(pl.BlockSpec((chunk, W_B), blk),
                   pl.BlockSpec((None, H_B, HEAD_DIM, HEAD_DIM), lambda b, c: (b, 0, 0, 0))),
        out_shape=(jax.ShapeDtypeStruct((batch * seq, W_B), BF16),
                   jax.ShapeDtypeStruct((batch, H_B, HEAD_DIM, HEAD_DIM), F32)),
        scratch_shapes=[pltpu.VMEM((H_B // 2, LANES, LANES), F32)],
        compiler_params=_cparams(("arbitrary", "arbitrary")),
        name="ret_prompt")(qb, kb, vb, gb, dmask, qdec, kdec, cdec, g_ret.reshape(1, W_B), avg)


FF_CHUNK = 1024


def _outffn_kernel(a_ref, b_ref, x_ref, wo_ref, gpost_ref, gfpre_ref, gfpost_ref,
                   wup_ref, wdn_ref, y_ref):
    m = (jnp.dot(a_ref[...], wo_ref[0:W_A, :], preferred_element_type=F32)
         + jnp.dot(b_ref[...], wo_ref[W_A:D_MODEL, :], preferred_element_type=F32))
    x1 = x_ref[...] + _rms(m, gpost_ref[...])
    hn = _rms(x1, gfpre_ref[...]).astype(BF16)
    f = jnp.zeros(x1.shape, F32)
    for c in range(D_FF // FF_CHUNK):
        cs = slice(c * FF_CHUNK, (c + 1) * FF_CHUNK)
        u = jnp.dot(hn, wup_ref[:, cs], preferred_element_type=F32)
        u = jnp.square(jnp.maximum(u, 0.0)).astype(BF16)
        f = f + jnp.dot(u, wdn_ref[cs, :], preferred_element_type=F32)
    y_ref[...] = x1 + _rms(f, gfpost_ref[...])


def _outffn_call(a, b, x, layer, wo, gpost, gfpre, gfpost, wup, wdn, tm):
    m = x.shape[0]
    row = lambda i: (i, 0)
    const = lambda i: (0, 0)
    slab = lambda i: (layer, 0, 0)
    once = dict(pipeline_mode=pl.Buffered(1))
    return pl.pallas_call(
        _outffn_kernel, grid=(m // tm,),
        in_specs=[pl.BlockSpec((tm, W_A), row), pl.BlockSpec((tm, W_B), row),
                  pl.BlockSpec((tm, D_MODEL), row),
                  pl.BlockSpec((None, D_MODEL, D_MODEL), slab, **once),
                  pl.BlockSpec((1, D_MODEL), const), pl.BlockSpec((1, D_MODEL), const),
                  pl.BlockSpec((1, D_MODEL), const),
                  pl.BlockSpec((None, D_MODEL, D_FF), slab, **once),
                  pl.BlockSpec((None, D_FF, D_MODEL), slab, **once)],
        out_specs=pl.BlockSpec((tm, D_MODEL), row),
        out_shape=jax.ShapeDtypeStruct((m, D_MODEL), F32),
        compiler_params=_cparams(("arbitrary",)),
        name="outffn")(a, b, x, wo, gpost.reshape(1, -1), gfpre.reshape(1, -1),
                       gfpost.reshape(1, -1), wup, wdn)


SCORE_SEQS = 8


def _dsa_sample_scores_kernel(pt_ref, *refs, n_pages):
    qi_ref, wi_ref, kinew_ref, o_ref = refs[SCORE_SEQS * n_pages:]
    lane = lax.broadcasted_iota(I32, (1, LANES), 1)
    for j in range(SCORE_SEQS):
        pages = refs[j * n_pages:(j + 1) * n_pages]
        qi = qi_ref[j]
        wi = wi_ref[j]

        def weighted(d):
            r = jnp.sum(wi * jnp.maximum(d, 0.0), axis=0, keepdims=True)
            return jnp.where(r == 0.0, 0.0, r)

        for p in range(n_pages):
            d = jnp.dot(qi, pages[p][...].astype(BF16), preferred_element_type=F32)
            o_ref[j, :, p * LANES:(p + 1) * LANES] = weighted(d)
        knew = kinew_ref[j].astype(BF16).astype(F32)
        d_new = jnp.sum(qi.astype(F32) * knew, axis=1, keepdims=True)
        o_ref[j, :, n_pages * LANES:(n_pages + 1) * LANES] = jnp.where(
            lane == 0, weighted(d_new), -jnp.inf)


def _dsa_sample_scores_call(page_table, pool_ki_t, layer, qi16, wi16, ki_new):
    nb, n_pages = page_table.shape
    ncol = (n_pages + 1) * LANES
    per = lambda b, pt: (b, 0, 0)
    page = lambda j, p: pl.BlockSpec((None, None, D_IDX, PAGE_SIZE),
                                     lambda b, pt: (layer, pt[b * SCORE_SEQS + j, p], 0, 0))
    return pl.pallas_call(
        functools.partial(_dsa_sample_scores_kernel, n_pages=n_pages),
        grid_spec=pltpu.PrefetchScalarGridSpec(
            num_scalar_prefetch=1, grid=(nb // SCORE_SEQS,),
            in_specs=[page(j, p) for j in range(SCORE_SEQS) for p in range(n_pages)] + [
                pl.BlockSpec((SCORE_SEQS, 16, D_IDX), per),
                pl.BlockSpec((SCORE_SEQS, 16, 1), per),
                pl.BlockSpec((SCORE_SEQS, 1, D_IDX), per)],
            out_specs=pl.BlockSpec((SCORE_SEQS, 1, ncol), per)),
        out_shape=jax.ShapeDtypeStruct((nb, 1, ncol), F32),
        compiler_params=_cparams(("arbitrary",)),
        name="dsa_sample_scores")(page_table, *([pool_ki_t] * (SCORE_SEQS * n_pages)),
                                  qi16, wi16, ki_new)


def _dsa_sample_select_kernel(sc_ref, sel_ref, sc_scr, pos_scr, *, topk, n_valid):
    ncol = sc_ref.shape[1]
    for p in range(ncol // LANES):
        rows = slice(p * LANES, (p + 1) * LANES)
        sc_scr[0, rows, :] = sc_ref[:, rows].T
    thr, cut = _topk_threshold(sc_scr, pos_scr, 1, ncol, topk)
    krow = lax.broadcasted_iota(I32, (LANES, LANES), 0)
    for p in range(ncol // LANES):
        rows = slice(p * LANES, (p + 1) * LANES)
        k = sc_scr[0, rows, :]
        kpos = krow + p * LANES
        sel = ((k > thr) | ((k == thr) & (kpos < cut))) & (kpos < n_valid)
        sel_ref[:, rows] = jnp.where(sel, 1.0, 0.0).T


def _dsa_sample_select_call(sc, topk, n_valid):
    nb, ncol = sc.shape
    return pl.pallas_call(
        functools.partial(_dsa_sample_select_kernel, topk=topk, n_valid=n_valid),
        out_shape=jax.ShapeDtypeStruct((nb, ncol), F32),
        scratch_shapes=[pltpu.VMEM((1, ncol, nb), F32), pltpu.VMEM((1, ncol, nb), F32)],
        compiler_params=pltpu.CompilerParams(vmem_limit_bytes=VMEM_LIMIT),
        name="dsa_sample_select")(sc)


def _dsa_sample_attn_kernel(pt_ref, *refs, n_pages):
    kpages = refs[:n_pages]
    vpages = refs[n_pages:2 * n_pages]
    q_ref, knew_ref, vnew_ref, sel_ref, o_ref = refs[2 * n_pages:]
    rows = lax.broadcasted_iota(I32, (16, W_A), 0)
    lanes = lax.broadcasted_iota(I32, (16, W_A), 1)
    own_head = lax.shift_right_logical(lanes, 6) == rows
    q_bd = jnp.where(own_head, jnp.broadcast_to(q_ref[...], (16, W_A)), 0.0).astype(BF16)
    first_pos = lax.broadcasted_iota(I32, (PAGE_SIZE, W_A), 0) == 0

    def new_tile(ref):
        return jnp.where(first_pos, jnp.broadcast_to(ref[...], (PAGE_SIZE, W_A)), 0.0).astype(BF16)

    def page_t(ref):
        return ref[...].reshape(W_A, PAGE_SIZE).astype(BF16)

    s_all = []
    for p in range(n_pages + 1):
        if p < n_pages:
            s = jnp.dot(q_bd, page_t(kpages[p]), preferred_element_type=F32)
        else:
            s = lax.dot_general(q_bd, new_tile(knew_ref), NT, preferred_element_type=F32)
        s_all.append(jnp.where(sel_ref[:, p * LANES:(p + 1) * LANES] > 0.5, s, NEG))
    m = s_all[0]
    for s in s_all[1:]:
        m = jnp.maximum(m, s)
    m = jnp.max(m, axis=1, keepdims=True)
    p_all = [jnp.exp2(s - m) for s in s_all]
    l = p_all[0]
    for pp in p_all[1:]:
        l = l + pp
    inv = 1.0 / jnp.sum(l, axis=1, keepdims=True)
    acc = jnp.zeros((16, W_A), F32)
    for p in range(n_pages + 1):
        pn = (p_all[p] * inv).astype(BF16)
        if p < n_pages:
            acc = acc + lax.dot_general(pn, page_t(vpages[p]), NT, preferred_element_type=F32)
        else:
            acc = acc + jnp.dot(pn, new_tile(vnew_ref), preferred_element_type=F32)
    o_ref[...] = jnp.sum(jnp.where(own_head, acc, 0.0), axis=0, keepdims=True)


def _dsa_sample_attn_call(page_table, pool_k_t, pool_v_t, layer, q, k_new, v_new, sel):
    nb, n_pages = page_table.shape
    per = lambda b, pt: (b, 0, 0)
    page = lambda p: pl.BlockSpec((None, None, H_A, HEAD_DIM, PAGE_SIZE),
                                  lambda b, pt: (layer, pt[b, p], 0, 0, 0))
    row_blk = pl.BlockSpec((None, 1, W_A), per)
    return pl.pallas_call(
        functools.partial(_dsa_sample_attn_kernel, n_pages=n_pages),
        grid_spec=pltpu.PrefetchScalarGridSpec(
            num_scalar_prefetch=1, grid=(nb,),
            in_specs=[page(p) for p in range(n_pages)] * 2 + [
                row_blk, row_blk, row_blk,
                pl.BlockSpec((None, 1, sel.shape[2]), per)],
            out_specs=row_blk),
        out_shape=jax.ShapeDtypeStruct((nb, 1, W_A), F32),
        compiler_params=_cparams(("arbitrary",)),
        name="dsa_sample_attn")(page_table, *([pool_k_t] * n_pages), *([pool_v_t] * n_pages),
                                q, k_new, v_new, sel)


def _ret_sample_kernel(q_ref, k_ref, v_ref, g_ref, s_ref, gret_ref, gam_ref, o_ref, sout_ref):
    v = v_ref[...]
    gam = gam_ref[...]

    def body(d, o):
        s_new = s_ref[d] * gam + k_ref[pl.ds(d, 1), :] * v
        sout_ref[d] = s_new
        return o + q_ref[pl.ds(d, 1), :] * s_new

    o = lax.fori_loop(0, HEAD_DIM, body, jnp.zeros(v.shape, F32))
    mu = jnp.mean(o, axis=0, keepdims=True)
    var = jnp.mean(jnp.square(o - mu), axis=0, keepdims=True)
    on = (o - mu) * lax.rsqrt(var + EPS)
    o_ref[...] = on * gret_ref[...] * _swish(g_ref[...])


def _ret_sample_call(qb, kb, vb, gb, state_t, layer, g_ret):
    nb = qb.shape[0]
    gam = jnp.exp(1.0 * _log_gamma())
    gam_b = jnp.broadcast_to(jnp.repeat(gam, HEAD_DIM)[:, None], (W_B, nb))
    gret_b = jnp.broadcast_to(g_ret[:, None], (W_B, nb))
    head = pl.BlockSpec((HEAD_DIM, nb), lambda h: (h, 0))
    o_t, s_new = pl.pallas_call(
        _ret_sample_kernel, grid=(H_B,),
        in_specs=[head, head, head, head,
                  pl.BlockSpec((None, None, HEAD_DIM, HEAD_DIM, nb), lambda h: (layer, h, 0, 0, 0)),
                  head, head],
        out_specs=(head, pl.BlockSpec((None, HEAD_DIM, HEAD_DIM, nb), lambda h: (h, 0, 0, 0))),
        out_shape=(jax.ShapeDtypeStruct((W_B, nb), F32),
                   jax.ShapeDtypeStruct((H_B, HEAD_DIM, HEAD_DIM, nb), F32)),
        compiler_params=_cparams(("arbitrary",)),
        name="ret_sample")(qb.T, kb.T, vb.T, gb.T, state_t, gret_b, gam_b)
    return o_t.T.astype(BF16), s_new


def _split_w_in_t(w_t):
    d = w_t.shape[1]
    starts = np.cumsum((0,) + COL_SIZES).tolist()
    qi = w_t[starts[3]:starts[4]]
    kiwi = w_t[starts[4]:starts[6]]
    qi_pad = jnp.pad(qi.reshape(H_IDX, D_IDX, d), ((0, 0), (0, LANES - D_IDX), (0, 0))).reshape(-1, d)
    w_mid = jnp.concatenate([qi_pad, kiwi, jnp.zeros((LANES - D_IDX - H_IDX, d), w_t.dtype)], axis=0)
    return w_mid, w_t[starts[6]:]


def _rope_tables(pos):
    half = HEAD_DIM // 2
    inv = ROPE_THETA ** (-jnp.arange(half, dtype=F32) / half)
    ang = pos[:, None] * inv[None, :]
    c = jnp.cos(ang)
    s = jnp.sin(ang)
    return jnp.concatenate([c, c, c, c], axis=1), jnp.concatenate([-s, s, -s, s], axis=1)


def kernel(x_prompt, x_sample, cache_k, cache_v, cache_kidx, state_ret, page_table, w_in, w_out,
           g_ret, g_mix_pre, g_mix_post, g_ffn_pre, g_ffn_post, w_up, w_down):
    batch, seq, _ = x_prompt.shape
    nb, dec_seq, _ = x_sample.shape
    depth = w_in.shape[0]
    n_pages = page_table.shape[1]
    past = n_pages * PAGE_SIZE
    assert dec_seq == 1 and seq % DSA_CH == 0 and nb == LANES
    tm_p = DSA_CH

    cos_p, sin_p = _rope_tables(jnp.arange(seq, dtype=F32))
    cos_s, sin_s = _rope_tables(past + jnp.arange(dec_seq, dtype=F32))
    cos_s = jnp.broadcast_to(cos_s, (nb, LANES))
    sin_s = jnp.broadcast_to(sin_s, (nb, LANES))
    pool_k_t = jnp.transpose(cache_k, (0, 1, 3, 4, 2))
    pool_v_t = jnp.transpose(cache_v, (0, 1, 3, 4, 2))
    pool_ki_t = jnp.swapaxes(cache_kidx, 2, 3)
    state_t = jnp.transpose(state_ret, (0, 2, 3, 4, 1))
    w_in_t = jnp.transpose(w_in, (2, 0, 1))
    wo_bf = w_out.astype(BF16)
    wup_bf = w_up.astype(BF16)
    wdn_bf = w_down.astype(BF16)
    topk_s = min(TOPK_MAX, (past + dec_seq) // 4)

    yp = x_prompt.reshape(batch * seq, D_MODEL)
    ys = x_sample.reshape(nb, D_MODEL)
    kp, vp, kip, sp, kss, vss, kis, ss = [], [], [], [], [], [], [], []
    for l in range(depth):
        w_t = w_in_t[:, l, :].astype(BF16)
        w_mid, w_tail = _split_w_in_t(w_t)
        gpre = g_mix_pre[l].reshape(1, D_MODEL)

        (qa, ka, ka_bf, va, va_t, qi, kiwi, kiwi_bf, ki, qb, kb, vb, gb) = _proj_call(
            yp, gpre, w_t, w_mid, w_tail, cos_p, sin_p, tm_p)
        a = _dsa_prompt_call(qa, qi, kiwi, ka_bf, va_t, kiwi_bf, batch, seq)
        bmix, s_new = _ret_prompt_call(qb, kb, vb, gb, g_ret[l], batch, seq)
        yp = _outffn_call(a, bmix, yp, l, wo_bf, g_mix_post[l], g_ffn_pre[l], g_ffn_post[l],
                          wup_bf, wdn_bf, 512)
        kp.append(ka); vp.append(va); kip.append(ki); sp.append(s_new)

        (qa, ka, ka_bf, va, va_t, qi, kiwi, kiwi_bf, ki, qb, kb, vb, gb) = _proj_call(
            ys, gpre, w_t, w_mid, w_tail, cos_s, sin_s, nb)
        qi16 = jnp.pad(qi.reshape(nb, H_IDX, LANES)[:, :, :D_IDX], ((0, 0), (0, 16 - H_IDX), (0, 0)))
        wi16 = jnp.pad(kiwi[:, D_IDX:D_IDX + H_IDX], ((0, 0), (0, 16 - H_IDX)))[:, :, None]
        sc = _dsa_sample_scores_call(page_table, pool_ki_t, l, qi16, wi16, ki.reshape(nb, 1, D_IDX))
        sel = _dsa_sample_select_call(sc.reshape(nb, -1), topk_s, past + dec_seq)
        a = _dsa_sample_attn_call(page_table, pool_k_t, pool_v_t, l,
                                  qa.astype(F32).reshape(nb, 1, W_A),
                                  ka.reshape(nb, 1, W_A), va.reshape(nb, 1, W_A),
                                  sel.reshape(nb, 1, -1))
        bmix, s_new = _ret_sample_call(qb, kb, vb, gb, state_t, l, g_ret[l])
        ys = _outffn_call(a.reshape(nb, W_A).astype(BF16), bmix, ys, l, wo_bf, g_mix_post[l],
                          g_ffn_pre[l], g_ffn_post[l], wup_bf, wdn_bf, nb)
        kss.append(ka); vss.append(va); kis.append(ki); ss.append(s_new)

    def heads(xs, lead):
        return jnp.stack(xs).reshape((depth,) + lead + (H_A, HEAD_DIM))

    ret_s = jnp.transpose(jnp.stack(ss), (0, 4, 1, 2, 3))
    return (yp.reshape(batch, seq, D_MODEL), ys.reshape(nb, dec_seq, D_MODEL),
            heads(kp, (batch, seq)), heads(vp, (batch, seq)),
            jnp.stack(kip).reshape(depth, batch, seq, D_IDX), jnp.stack(sp),
            heads(kss, (nb, dec_seq)), heads(vss, (nb, dec_seq)),
            jnp.stack(kis).reshape(depth, nb, dec_seq, D_IDX), ret_s)
```

```python
import functools

import numpy as np
import jax
import jax.numpy as jnp
from jax import lax
from jax.experimental import pallas as pl
from jax.experimental.pallas import tpu as pltpu

F32 = jnp.float32
BF16 = jnp.bfloat16
I32 = jnp.int32

D_MODEL = 1024
HEAD_DIM = 64
H_A = 8
H_B = 8
H_IDX = 4
D_IDX = 64
W_A = H_A * HEAD_DIM
W_B = H_B * HEAD_DIM
D_FF = 4 * D_MODEL
TOPK_MAX = 256
PAGE_SIZE = 128
ROPE_THETA = 10000.0
EPS = 1e-6
Q_BLOCK = 128
RET_CHUNK = 128
RET_STEP = 2
COL_SIZES = (W_A, W_A, W_A, H_IDX * D_IDX, D_IDX, H_IDX, W_B, W_B, W_B, W_B)

LANES = 128
NEG = -1e30
VMEM_LIMIT = 48 * 1024 * 1024

LOG2E = 1.4426950408889634
VT_ROWS = 144
VT_ALL = (H_A // 2) * VT_ROWS

NT = (((1,), (1,)), ((), ()))
TN = (((0,), (0,)), ((), ()))

C_QA, C_KA, C_VA, C_QI, C_KIWI, C_QB, C_KB, C_VB, C_GB = (
    0, 512, 1024, 1536, 2048, 2176, 2688, 3200, 3712)
N_PACK = 4224


def _cparams(sem):
    return pltpu.CompilerParams(dimension_semantics=sem, vmem_limit_bytes=VMEM_LIMIT)


def _rms(x, g):
    return x * lax.rsqrt(jnp.mean(x * x, axis=-1, keepdims=True) + EPS) * g


def _proj_kernel(x_ref, g_ref, whead_ref, wmid_ref, wtail_ref, cos_ref, sin_ref,
                 qa_ref, ka_ref, kabf_ref, va_ref, vat_ref, qi_ref, kiwi_ref, kiwibf_ref,
                 ki_ref, qb_ref, kb_ref, vb_ref, gb_ref):
    x = x_ref[...]
    h = _rms(x, g_ref[...]).astype(BF16)
    cos = cos_ref[...]
    sin = sin_ref[...]
    lane = lax.broadcasted_iota(I32, cos.shape, 1)
    first_half = (lane % 64) < 32

    def rope(z, c, s):
        sw = jnp.where(first_half, pltpu.roll(z, 96, 1), pltpu.roll(z, 32, 1))
        return z * c + sw * s

    def proj(c0, n):
        if c0 < C_QI:
            w = whead_ref[c0:c0 + n, :]
        elif c0 < C_QB:
            w = wmid_ref[c0 - C_QI:c0 - C_QI + n, :]
        else:
            w = wtail_ref[c0 - C_QB:c0 - C_QB + n, :]
        return lax.dot_general(h, w, NT, preferred_element_type=F32)

    def roped(c0, scale):
        z = proj(c0, 512)
        parts = []
        for j in range(4):
            r = rope(z[:, j * LANES:(j + 1) * LANES], cos, sin)
            parts.append(r if scale is None else r * scale)
        return parts

    qa = roped(C_QA, HEAD_DIM ** -0.5 * LOG2E)
    for j in range(4):
        qa_ref[:, j * LANES:(j + 1) * LANES] = qa[j].astype(BF16)
    ka = roped(C_KA, None)
    for j in range(4):
        ka_ref[:, j * LANES:(j + 1) * LANES] = ka[j]
        kabf_ref[:, j * LANES:(j + 1) * LANES] = ka[j].astype(BF16)
    va = proj(C_VA, 512)
    va_ref[...] = va
    va_t = va.T
    ones = jnp.ones((VT_ROWS - LANES, va_t.shape[1]), BF16)
    for p in range(H_A // 2):
        vat_ref[p * VT_ROWS:p * VT_ROWS + LANES, :] = va_t[p * LANES:(p + 1) * LANES, :].astype(BF16)
        vat_ref[p * VT_ROWS + LANES:(p + 1) * VT_ROWS, :] = ones
    qi = roped(C_QI, None)
    for j in range(4):
        qi_ref[:, j * LANES:(j + 1) * LANES] = qi[j].astype(BF16)
    zk = proj(C_KIWI, LANES)
    is_key = lane < 64
    kiwi = rope(zk, jnp.where(is_key, cos, (H_IDX ** -0.5) * (D_IDX ** -0.5)),
                jnp.where(is_key, sin, 0.0))
    kiwi_ref[...] = kiwi
    kiwibf_ref[...] = kiwi.astype(BF16)
    ki_ref[...] = kiwi[:, 0:64]
    qb = roped(C_QB, None)
    for j in range(4):
        qb_ref[:, j * LANES:(j + 1) * LANES] = qb[j]
    kb = roped(C_KB, HEAD_DIM ** -0.5)
    for j in range(4):
        kb_ref[:, j * LANES:(j + 1) * LANES] = kb[j]
    vb_ref[...] = proj(C_VB, 512)
    gb_ref[...] = proj(C_GB, 512)


def _proj_call(x, g, w_t, w_mid, w_tail, cos_t, sin_t, tm):
    m = x.shape[0]
    once = dict(pipeline_mode=pl.Buffered(1))
    n_t = cos_t.shape[0] // tm
    row = lambda i: (i, 0)
    const = lambda i: (0, 0)
    tab = lambda i: (i % n_t, 0)
    wide = lambda dt: jax.ShapeDtypeStruct((m, 512), dt)
    out_shape = (wide(BF16), wide(F32), wide(BF16), wide(F32),
                 jax.ShapeDtypeStruct((m // tm, VT_ALL, tm), BF16), wide(BF16),
                 jax.ShapeDtypeStruct((m, LANES), F32), jax.ShapeDtypeStruct((m, LANES), BF16),
                 jax.ShapeDtypeStruct((m, 64), F32), wide(F32), wide(F32), wide(F32), wide(F32))
    out_specs = tuple(
        pl.BlockSpec((None, VT_ALL, tm), lambda i: (i, 0, 0)) if len(s.shape) == 3
        else pl.BlockSpec((tm, s.shape[1]), row) for s in out_shape)
    return pl.pallas_call(
        _proj_kernel, grid=(m // tm,),
        in_specs=[pl.BlockSpec((tm, D_MODEL), row),
                  pl.BlockSpec((1, D_MODEL), const),
                  pl.BlockSpec((C_QI, D_MODEL), const, **once),
                  pl.BlockSpec((C_QB - C_QI, D_MODEL), const, **once),
                  pl.BlockSpec((N_PACK - C_QB, D_MODEL), const, **once),
                  pl.BlockSpec((tm, LANES), tab),
                  pl.BlockSpec((tm, LANES), tab)],
        out_specs=out_specs, out_shape=out_shape,
        compiler_params=_cparams(("arbitrary",)), name="proj")(x, g, w_t, w_mid, w_tail, cos_t, sin_t)


COUNT_ROWS = 32
KEY_NEG_INF = -2139095041
RANK_TOP = 65535
BIG = 3e38


def _key_to_float(key):
    bits = key ^ ((key >> 31) & 0x7FFFFFFF)
    return jnp.where(key <= KEY_NEG_INF, -jnp.inf, pltpu.bitcast(bits, F32))


def _count(scr, nc, ch, pred):
    def body(c, acc):
        m = jnp.where(pred(scr[c], c), 1.0, 0.0)
        for r in range(ch // COUNT_ROWS):
            acc = acc + m[r * COUNT_ROWS:(r + 1) * COUNT_ROWS]
        return acc

    acc = lax.fori_loop(0, nc, body, jnp.zeros((COUNT_ROWS, LANES), F32))
    return jnp.sum(acc, axis=0, keepdims=True)


def _bit_search(scr, nc, ch, target, init, nbits, cand_float):
    def bit_body(j, carry):
        lo, lo_cnt, up_cnt = carry
        cand = lo + jnp.left_shift(jnp.int32(1), nbits - 1 - j)
        cand_f = cand_float(cand)
        cnt = _count(scr, nc, ch, lambda s, c: s >= cand_f)
        ok = cnt >= target
        return jnp.where(ok, cand, lo), jnp.where(ok, cnt, lo_cnt), jnp.where(ok, up_cnt, cnt)

    return lax.fori_loop(0, nbits, bit_body, init)


def _topk_threshold(sc_scr, pos_scr, nc, ch, topk):
    assert ch >= topk and ch % COUNT_ROWS == 0
    kf = float(topk)
    total = (nc * ch).astype(F32) if hasattr(nc, "astype") else float(nc * ch)
    cnt0 = _count(sc_scr, nc, ch, lambda s, c: s >= 0.0)
    ok0 = cnt0 >= kf
    init = (jnp.where(ok0, 0, -32768), jnp.where(ok0, cnt0, total), jnp.where(ok0, 0.0, cnt0))
    t_hi, at_hi, above_hi = _bit_search(sc_scr, nc, ch, kf, init, 15,
                                        lambda v: _key_to_float(v << 16))
    need_lo = kf - above_hi
    bucket_n = at_hi - above_hi
    bucket_lo = _key_to_float(t_hi << 16)
    bucket_up = _key_to_float((t_hi + 1) << 16)

    def top_body(c, top):
        s = sc_scr[c]
        s_b = jnp.where((s >= bucket_lo) & jnp.logical_not(s >= bucket_up), s, -jnp.inf)
        for r in range(ch // COUNT_ROWS):
            top = jnp.maximum(top, s_b[r * COUNT_ROWS:(r + 1) * COUNT_ROWS])
        return top

    top = lax.fori_loop(0, nc, top_body, jnp.full((COUNT_ROWS, LANES), -jnp.inf, F32))
    v_top = jnp.max(top, axis=0, keepdims=True)
    n_top = _count(sc_scr, nc, ch, lambda s, c: s == v_top)
    one_key = (n_top == bucket_n) & (bucket_n > need_lo)
    krow = lax.broadcasted_iota(I32, (ch, LANES), 0)

    @pl.when(jnp.max(jnp.where(one_key, 1.0, 0.0)) > 0.0)
    def _():
        def rank_body(c, carry):
            s = sc_scr[c]
            rank = (RANK_TOP - (krow + c * ch)).astype(F32)
            ranked = jnp.where(s > v_top, BIG, jnp.where(s == v_top, rank, -BIG))
            sc_scr[c] = jnp.where(one_key, ranked, s)
            return carry

        lax.fori_loop(0, nc, rank_body, 0)

    def second_level(v):
        return jnp.where(one_key, v.astype(F32), _key_to_float((t_hi << 16) | v))

    t_lo, at_lo, above_lo = _bit_search(sc_scr, nc, ch, kf,
                                        (jnp.zeros_like(t_hi), at_hi, above_hi), 16, second_level)
    thr = second_level(t_lo)
    need_tie = kf - above_lo

    n_tied = at_lo - above_lo

    def tie_search():
        def tie_prep(c, carry):
            pos_scr[c] = jnp.where(sc_scr[c] == thr, (krow + c * ch).astype(F32), BIG)
            return carry

        lax.fori_loop(0, nc, tie_prep, 0)

        def tie_body(j, cut):
            cand = cut + jnp.left_shift(jnp.int32(1), 12 - j)
            cand_f = cand.astype(F32)
            f = _count(pos_scr, nc, ch, lambda p, c: p < cand_f)
            return jnp.where(f <= need_tie, cand, cut)

        return lax.fori_loop(0, 13, tie_body, jnp.zeros((1, LANES), I32))

    cut = lax.cond(jnp.max(n_tied - need_tie) > 0.0, tie_search,
                   lambda: jnp.full((1, LANES), 1 << 30, I32))
    return thr, cut


DSA_CH = 512


def _dsa_prompt_kernel(qa_ref, qi_ref, kiwi_ref, ka_ref, vat_ref, kibf_ref, o_ref,
                       sc_scr, pos_scr, bias_scr, acc_scr, qcat_scr, qicat_scr,
                       sa_scr, sb_scr, *, topk):
    ch = DSA_CH
    n_pair = H_A // 2
    i = pl.program_id(1)
    t0 = i * Q_BLOCK
    nc = lax.div(t0 + Q_BLOCK + ch - 1, ch)
    qpos = t0 + lax.broadcasted_iota(I32, (ch, LANES), 1)
    krow = lax.broadcasted_iota(I32, (ch, LANES), 0)
    wi_t = kiwi_ref[...].T

    d_lo = lax.broadcasted_iota(I32, (LANES, Q_BLOCK), 0) < HEAD_DIM
    for p in range(n_pair):
        qt = qa_ref[:, p * LANES:(p + 1) * LANES].astype(F32).T
        qcat_scr[p, :, 0:Q_BLOCK] = jnp.where(d_lo, qt, 0.0).astype(BF16)
        qcat_scr[p, :, Q_BLOCK:2 * Q_BLOCK] = jnp.where(d_lo, 0.0, qt).astype(BF16)
    for g in range(H_IDX // 2):
        for e in range(2):
            h = 2 * g + e
            qicat_scr[g, :, e * Q_BLOCK:(e + 1) * Q_BLOCK] = (
                qi_ref[:, h * LANES:(h + 1) * LANES].astype(F32).T.astype(BF16))

    def score_body(c, carry):
        off = pl.multiple_of(c * ch, ch)
        kic = kibf_ref[pl.ds(off, ch), :]
        acc = jnp.zeros((ch, LANES), F32)
        for g in range(H_IDX // 2):
            d = jnp.dot(kic, qicat_scr[g], preferred_element_type=F32)
            acc = (acc + wi_t[64 + 2 * g:65 + 2 * g, :] * jnp.maximum(d[:, 0:LANES], 0.0)
                   + wi_t[65 + 2 * g:66 + 2 * g, :] * jnp.maximum(d[:, LANES:2 * LANES], 0.0))
        acc = jnp.where(acc == 0.0, 0.0, acc)
        sc_scr[c] = jnp.where(krow + c * ch <= qpos, acc, -jnp.inf)
        return carry

    lax.fori_loop(0, nc, score_body, 0)
    thr, cut = _topk_threshold(sc_scr, pos_scr, nc, ch, topk)
    odd = (nc & 1) == 1

    def bias_body(c, carry):
        k = sc_scr[c]
        kpos = krow + c * ch
        sel = ((k > thr) | ((k == thr) & (kpos < cut))) & (kpos <= qpos)
        bias_scr[c] = jnp.where(sel, 0.0, NEG)
        return carry

    lax.fori_loop(0, nc, bias_body, 0)

    row_lo = lax.broadcasted_iota(I32, (LANES, Q_BLOCK), 0) < HEAD_DIM
    acc_scr[...] = jnp.zeros_like(acc_scr)
    last_chunk = ka_ref.shape[0] // ch - 1

    def qk_scores(c, p):
        off = pl.multiple_of(jnp.minimum(c, last_chunk) * ch, ch)
        return jnp.dot(ka_ref[pl.ds(off, ch), p * LANES:(p + 1) * LANES], qcat_scr[p],
                       preferred_element_type=F32)

    for p in range(n_pair):
        sa_scr[p] = qk_scores(0, p)

    def half_step(c, cur_scr, nxt_scr, carry, prefetch=True):
        b = bias_scr[c]
        new = []
        if prefetch:
            for p in range(n_pair):
                nxt_scr[p] = qk_scores(c + 1, p)
        for p in range(n_pair):
            s = cur_scr[p]
            alphas, pts, ms = [], [], []
            for e in range(2):
                m = carry[4 * p + 2 * e]
                sb = s[:, e * LANES:(e + 1) * LANES] + b
                m_new = jnp.maximum(m, jnp.max(sb, axis=0, keepdims=True))
                alphas.append(jnp.exp2(m - m_new))
                pts.append(jnp.exp2(sb - m_new).astype(BF16))
                ms.append(m_new)
            pv = jnp.dot(vat_ref[c, p * VT_ROWS:(p + 1) * VT_ROWS, :],
                         jnp.concatenate(pts, axis=1), preferred_element_type=F32)
            for e in range(2):
                l = carry[4 * p + 2 * e + 1]
                new += [ms[e], alphas[e] * l + pv[LANES:LANES + 1, e * LANES:(e + 1) * LANES]]
            acc_scr[p] = (acc_scr[p] * jnp.where(row_lo, alphas[0], alphas[1])
                          + jnp.where(row_lo, pv[0:LANES, 0:LANES], pv[0:LANES, LANES:2 * LANES]))
        return tuple(new)

    def att_body(j, carry):
        carry = half_step(2 * j, sa_scr, sb_scr, carry)
        return half_step(2 * j + 1, sb_scr, sa_scr, carry)

    init = (jnp.full((1, LANES), NEG, F32), jnp.zeros((1, LANES), F32)) * H_A
    stats = lax.fori_loop(0, lax.shift_right_logical(nc, 1), att_body, init)
    stats = lax.cond(odd, lambda st: half_step(nc - 1, sa_scr, sb_scr, st, prefetch=False),
                     lambda st: st, stats)
    for p in range(n_pair):
        l_pair = jnp.where(row_lo, stats[4 * p + 1], stats[4 * p + 3])
        o_ref[:, p * LANES:(p + 1) * LANES] = (acc_scr[p] / l_pair).T.astype(BF16)


def _dsa_prompt_call(qa, qi, kiwi, ka_bf, va_t, kiwi_bf, batch, seq):
    nq = seq // Q_BLOCK
    topk = min(TOPK_MAX, seq // 4)
    nch = seq // DSA_CH
    blk = lambda b, i: (b * nq + i, 0)
    full = lambda b, i: (b, 0)
    return pl.pallas_call(
        functools.partial(_dsa_prompt_kernel, topk=topk), grid=(batch, nq),
        in_specs=[pl.BlockSpec((Q_BLOCK, W_A), blk),
                  pl.BlockSpec((Q_BLOCK, 512), blk),
                  pl.BlockSpec((Q_BLOCK, LANES), blk),
                  pl.BlockSpec((seq, W_A), full),
                  pl.BlockSpec((nch, VT_ALL, DSA_CH), lambda b, i: (b, 0, 0)),
                  pl.BlockSpec((seq, LANES), full)],
        out_specs=pl.BlockSpec((Q_BLOCK, W_A), blk),
        out_shape=jax.ShapeDtypeStruct((batch * seq, W_A), BF16),
        scratch_shapes=[pltpu.VMEM((nch, DSA_CH, LANES), F32),
                        pltpu.VMEM((nch, DSA_CH, LANES), F32),
                        pltpu.VMEM((nch, DSA_CH, LANES), F32),
                        pltpu.VMEM((H_A // 2, LANES, Q_BLOCK), F32),
                        pltpu.VMEM((H_A // 2, LANES, 2 * Q_BLOCK), BF16),
                        pltpu.VMEM((H_IDX // 2, LANES, 2 * Q_BLOCK), BF16),
                        pltpu.VMEM((H_A // 2, DSA_CH, 2 * Q_BLOCK), F32),
                        pltpu.VMEM((H_A // 2, DSA_CH, 2 * Q_BLOCK), F32)],
        compiler_params=_cparams(("arbitrary", "arbitrary")),
        name="dsa_prompt")(qa, qi, kiwi, ka_bf, va_t, kiwi_bf)


def _swish(g):
    return g * (1.0 / (1.0 + jnp.exp(-g)))


def _ret_prompt_kernel(q_ref, k_ref, v_ref, g_ref, dmask_ref, qdec_ref, kdec_ref, cdec_ref,
                       gret_ref, avg_ref, o_ref, sout_ref, s_scr):
    c = pl.program_id(1)

    @pl.when(c == 0)
    def _():
        s_scr[...] = jnp.zeros_like(s_scr)

    lane_lo = lax.broadcasted_iota(I32, (RET_CHUNK, LANES), 1) < HEAD_DIM
    avg = avg_ref[...]
    same_head = (lax.broadcasted_iota(I32, (LANES, LANES), 0) < HEAD_DIM) == (
        lax.broadcasted_iota(I32, (LANES, LANES), 1) < HEAD_DIM)
    pairs = range(H_B // 2)
    cols = [slice(p * LANES, (p + 1) * LANES) for p in pairs]

    def head_mean(x):
        x_hi = x.astype(BF16)
        x_lo = (x - x_hi.astype(F32)).astype(BF16)
        return (jnp.dot(x_hi, avg, preferred_element_type=F32)
                + jnp.dot(x_lo, avg, preferred_element_type=F32))

    for sub in range(RET_STEP):
        rows = slice(sub * RET_CHUNK, (sub + 1) * RET_CHUNK)
        qb = [q_ref[rows, cols[p]].astype(BF16) for p in pairs]
        kb = [k_ref[rows, cols[p]].astype(BF16) for p in pairs]
        vb = [v_ref[rows, cols[p]].astype(BF16) for p in pairs]
        att = []
        for p in pairs:
            zero = jnp.zeros_like(qb[p])
            for e, qe in enumerate((jnp.where(lane_lo, qb[p], zero), jnp.where(lane_lo, zero, qb[p]))):
                att.append(lax.dot_general(qe, kb[p], NT, preferred_element_type=F32)
                           * dmask_ref[2 * p + e])
        o = []
        for p in pairs:
            intra = [jnp.dot(att[2 * p + e].astype(BF16), vb[p], preferred_element_type=F32)
                     for e in range(2)]
            s_bd = s_scr[p]
            o.append(jnp.where(lane_lo, intra[0], intra[1])
                     + jnp.dot(qb[p], s_bd.astype(BF16), preferred_element_type=F32)
                     * qdec_ref[:, cols[p]])
            kd = (k_ref[rows, cols[p]] * kdec_ref[:, cols[p]]).astype(BF16)
            upd = lax.dot_general(kd, vb[p], TN, preferred_element_type=F32)
            s_scr[p] = s_bd * cdec_ref[p] + jnp.where(same_head, upd, 0.0)
        dev = [o[p] - head_mean(o[p]) for p in pairs]
        var = [head_mean(dev[p] * dev[p]) for p in pairs]
        for p in pairs:
            on = dev[p] * lax.rsqrt(var[p] + EPS)
            o_ref[rows, cols[p]] = (on * gret_ref[:, cols[p]]
                                    * _swish(g_ref[rows, cols[p]])).astype(BF16)

    @pl.when(c == pl.num_programs(1) - 1)
    def _():
        for p in range(H_B // 2):
            s_bd = s_scr[p]
            sout_ref[2 * p] = s_bd[0:HEAD_DIM, 0:HEAD_DIM]
            sout_ref[2 * p + 1] = s_bd[HEAD_DIM:LANES, HEAD_DIM:LANES]


def _log_gamma():
    return jnp.log1p(-jnp.exp2(-5.0 - jnp.arange(H_B, dtype=F32)))


def _ret_tables(chunk):
    lg = _log_gamma()
    i = jnp.arange(chunk, dtype=F32)
    diff = i[:, None] - i[None, :]
    dmask = jnp.where(diff[None] >= 0, jnp.exp(jnp.maximum(diff, 0.0)[None] * lg[:, None, None]), 0.0)
    q_dec = jnp.exp((i[:, None] + 1.0) * lg[None, :])
    k_dec = jnp.exp((chunk - 1.0 - i)[:, None] * lg[None, :])
    c_dec = jnp.exp(chunk * lg)
    return dmask, q_dec, k_dec, c_dec


def _ret_prompt_call(qb, kb, vb, gb, g_ret, batch, seq):
    chunk = RET_CHUNK
    n = seq // chunk
    dmask, q_dec, k_dec, c_dec = _ret_tables(chunk)
    qdec = jnp.repeat(q_dec, HEAD_DIM, axis=1)
    kdec = jnp.repeat(k_dec, HEAD_DIM, axis=1)
    cdec = jnp.broadcast_to(jnp.repeat(c_dec, HEAD_DIM).reshape(H_B // 2, LANES, 1),
                            (H_B // 2, LANES, LANES))
    head_of = jnp.arange(LANES) // HEAD_DIM
    avg = jnp.where(head_of[:, None] == head_of[None, :], 1.0 / HEAD_DIM, 0.0).astype(BF16)
    assert n % RET_STEP == 0
    steps = n // RET_STEP
    blk = lambda b, c: (b * steps + c, 0)
    c2 = lambda b, c: (0, 0)
    c3 = lambda b, c: (0, 0, 0)
    return pl.pallas_call(
        _ret_prompt_kernel, grid=(batch, steps),
        in_specs=[pl.BlockSpec((RET_STEP * chunk, W_B), blk)] * 4 + [
            pl.BlockSpec((H_B, chunk, chunk), c3),
            pl.BlockSpec((chunk, W_B), c2),
            pl.BlockSpec((chunk, W_B), c2),
            pl.BlockSpec((H_B // 2, LANES, LANES), c3),
            pl.BlockSpec((1, W_B), c2),
            pl.BlockSpec((LANES, LANES), c2)],
        out_specs=(pl.BlockSpec((RET_STEP * chunk, W_B), blk),
                   pl.BlockSpec((None, H_B, HEAD_DIM, HEAD_DIM), lambda b, c: (b, 0, 0, 0))),
        out_shape=(jax.ShapeDtypeStruct((batch * seq, W_B), BF16),
                   jax.ShapeDtypeStruct((batch, H_B, HEAD_DIM, HEAD_DIM), F32)),
        scratch_shapes=[pltpu.VMEM((H_B // 2, LANES, LANES), F32)],
        compiler_params=_cparams(("arbitrary", "arbitrary")),
        name="ret_prompt")(qb, kb, vb, gb, dmask, qdec, kdec, cdec, g_ret.reshape(1, W_B), avg)


FF_CHUNK = 1024


def _outffn_kernel(a_ref, b_ref, x_ref, wo_ref, gpost_ref, gfpre_ref, gfpost_ref,
                   wup_ref, wdn_ref, y_ref):
    m = (jnp.dot(a_ref[...], wo_ref[0:W_A, :], preferred_element_type=F32)
         + jnp.dot(b_ref[...], wo_ref[W_A:D_MODEL, :], preferred_element_type=F32))
    x1 = x_ref[...] + _rms(m, gpost_ref[...])
    hn = _rms(x1, gfpre_ref[...]).astype(BF16)
    f = jnp.zeros(x1.shape, F32)
    for c in range(D_FF // FF_CHUNK):
        cs = slice(c * FF_CHUNK, (c + 1) * FF_CHUNK)
        u = jnp.dot(hn, wup_ref[:, cs], preferred_element_type=F32)
        u = jnp.square(jnp.maximum(u, 0.0)).astype(BF16)
        f = f + jnp.dot(u, wdn_ref[cs, :], preferred_element_type=F32)
    y_ref[...] = x1 + _rms(f, gfpost_ref[...])


def _outffn_call(a, b, x, layer, wo, gpost, gfpre, gfpost, wup, wdn, tm):
    m = x.shape[0]
    row = lambda i: (i, 0)
    const = lambda i: (0, 0)
    slab = lambda i: (layer, 0, 0)
    once = dict(pipeline_mode=pl.Buffered(1))
    return pl.pallas_call(
        _outffn_kernel, grid=(m // tm,),
        in_specs=[pl.BlockSpec((tm, W_A), row), pl.BlockSpec((tm, W_B), row),
                  pl.BlockSpec((tm, D_MODEL), row),
                  pl.BlockSpec((None, D_MODEL, D_MODEL), slab, **once),
                  pl.BlockSpec((1, D_MODEL), const), pl.BlockSpec((1, D_MODEL), const),
                  pl.BlockSpec((1, D_MODEL), const),
                  pl.BlockSpec((None, D_MODEL, D_FF), slab, **once),
                  pl.BlockSpec((None, D_FF, D_MODEL), slab, **once)],
        out_specs=pl.BlockSpec((tm, D_MODEL), row),
        out_shape=jax.ShapeDtypeStruct((m, D_MODEL), F32),
        compiler_params=_cparams(("arbitrary",)),
        name="outffn")(a, b, x, wo, gpost.reshape(1, -1), gfpre.reshape(1, -1),
                       gfpost.reshape(1, -1), wup, wdn)


SCORE_SEQS = 8


def _dsa_sample_scores_kernel(pt_ref, *refs, n_pages):
    qi_ref, wi_ref, kinew_ref, o_ref = refs[SCORE_SEQS * n_pages:]
    lane = lax.broadcasted_iota(I32, (1, LANES), 1)
    for j in range(SCORE_SEQS):
        pages = refs[j * n_pages:(j + 1) * n_pages]
        qi = qi_ref[j]
        wi = wi_ref[j]

        def weighted(d):
            r = jnp.sum(wi * jnp.maximum(d, 0.0), axis=0, keepdims=True)
            return jnp.where(r == 0.0, 0.0, r)

        for p in range(n_pages):
            d = jnp.dot(qi, pages[p][...].astype(BF16), preferred_element_type=F32)
            o_ref[j, :, p * LANES:(p + 1) * LANES] = weighted(d)
        knew = kinew_ref[j].astype(BF16).astype(F32)
        d_new = jnp.sum(qi.astype(F32) * knew, axis=1, keepdims=True)
        o_ref[j, :, n_pages * LANES:(n_pages + 1) * LANES] = jnp.where(
            lane == 0, weighted(d_new), -jnp.inf)


def _dsa_sample_scores_call(page_table, pool_ki_t, layer, qi16, wi16, ki_new):
    nb, n_pages = page_table.shape
    ncol = (n_pages + 1) * LANES
    per = lambda b, pt: (b, 0, 0)
    page = lambda j, p: pl.BlockSpec((None, None, D_IDX, PAGE_SIZE),
                                     lambda b, pt: (layer, pt[b * SCORE_SEQS + j, p], 0, 0))
    return pl.pallas_call(
        functools.partial(_dsa_sample_scores_kernel, n_pages=n_pages),
        grid_spec=pltpu.PrefetchScalarGridSpec(
            num_scalar_prefetch=1, grid=(nb // SCORE_SEQS,),
            in_specs=[page(j, p) for j in range(SCORE_SEQS) for p in range(n_pages)] + [
                pl.BlockSpec((SCORE_SEQS, 16, D_IDX), per),
                pl.BlockSpec((SCORE_SEQS, 16, 1), per),
                pl.BlockSpec((SCORE_SEQS, 1, D_IDX), per)],
            out_specs=pl.BlockSpec((SCORE_SEQS, 1, ncol), per)),
        out_shape=jax.ShapeDtypeStruct((nb, 1, ncol), F32),
        compiler_params=_cparams(("arbitrary",)),
        name="dsa_sample_scores")(page_table, *([pool_ki_t] * (SCORE_SEQS * n_pages)),
                                  qi16, wi16, ki_new)


def _dsa_sample_select_kernel(sc_ref, sel_ref, sc_scr, pos_scr, *, topk, n_valid):
    ncol = sc_ref.shape[1]
    for p in range(ncol // LANES):
        rows = slice(p * LANES, (p + 1) * LANES)
        sc_scr[0, rows, :] = sc_ref[:, rows].T
    thr, cut = _topk_threshold(sc_scr, pos_scr, 1, ncol, topk)
    krow = lax.broadcasted_iota(I32, (LANES, LANES), 0)
    for p in range(ncol // LANES):
        rows = slice(p * LANES, (p + 1) * LANES)
        k = sc_scr[0, rows, :]
        kpos = krow + p * LANES
        sel = ((k > thr) | ((k == thr) & (kpos < cut))) & (kpos < n_valid)
        sel_ref[:, rows] = jnp.where(sel, 1.0, 0.0).T


def _dsa_sample_select_call(sc, topk, n_valid):
    nb, ncol = sc.shape
    return pl.pallas_call(
        functools.partial(_dsa_sample_select_kernel, topk=topk, n_valid=n_valid),
        out_shape=jax.ShapeDtypeStruct((nb, ncol), F32),
        scratch_shapes=[pltpu.VMEM((1, ncol, nb), F32), pltpu.VMEM((1, ncol, nb), F32)],
        compiler_params=pltpu.CompilerParams(vmem_limit_bytes=VMEM_LIMIT),
        name="dsa_sample_select")(sc)


def _dsa_sample_attn_kernel(pt_ref, *refs, n_pages):
    kpages = refs[:n_pages]
    vpages = refs[n_pages:2 * n_pages]
    q_ref, knew_ref, vnew_ref, sel_ref, o_ref = refs[2 * n_pages:]
    rows = lax.broadcasted_iota(I32, (16, W_A), 0)
    lanes = lax.broadcasted_iota(I32, (16, W_A), 1)
    own_head = lax.shift_right_logical(lanes, 6) == rows
    q_bd = jnp.where(own_head, jnp.broadcast_to(q_ref[...], (16, W_A)), 0.0).astype(BF16)
    first_pos = lax.broadcasted_iota(I32, (PAGE_SIZE, W_A), 0) == 0

    def new_tile(ref):
        return jnp.where(first_pos, jnp.broadcast_to(ref[...], (PAGE_SIZE, W_A)), 0.0).astype(BF16)

    def page_t(ref):
        return ref[...].reshape(W_A, PAGE_SIZE).astype(BF16)

    s_all = []
    for p in range(n_pages + 1):
        if p < n_pages:
            s = jnp.dot(q_bd, page_t(kpages[p]), preferred_element_type=F32)
        else:
            s = lax.dot_general(q_bd, new_tile(knew_ref), NT, preferred_element_type=F32)
        s_all.append(jnp.where(sel_ref[:, p * LANES:(p + 1) * LANES] > 0.5, s, NEG))
    m = s_all[0]
    for s in s_all[1:]:
        m = jnp.maximum(m, s)
    m = jnp.max(m, axis=1, keepdims=True)
    p_all = [jnp.exp2(s - m) for s in s_all]
    l = p_all[0]
    for pp in p_all[1:]:
        l = l + pp
    inv = 1.0 / jnp.sum(l, axis=1, keepdims=True)
    acc = jnp.zeros((16, W_A), F32)
    for p in range(n_pages + 1):
        pn = (p_all[p] * inv).astype(BF16)
        if p < n_pages:
            acc = acc + lax.dot_general(pn, page_t(vpages[p]), NT, preferred_element_type=F32)
        else:
            acc = acc + jnp.dot(pn, new_tile(vnew_ref), preferred_element_type=F32)
    o_ref[...] = jnp.sum(jnp.where(own_head, acc, 0.0), axis=0, keepdims=True)


def _dsa_sample_attn_call(page_table, pool_k_t, pool_v_t, layer, q, k_new, v_new, sel):
    nb, n_pages = page_table.shape
    per = lambda b, pt: (b, 0, 0)
    page = lambda p: pl.BlockSpec((None, None, H_A, HEAD_DIM, PAGE_SIZE),
                                  lambda b, pt: (layer, pt[b, p], 0, 0, 0))
    row_blk = pl.BlockSpec((None, 1, W_A), per)
    return pl.pallas_call(
        functools.partial(_dsa_sample_attn_kernel, n_pages=n_pages),
        grid_spec=pltpu.PrefetchScalarGridSpec(
            num_scalar_prefetch=1, grid=(nb,),
            in_specs=[page(p) for p in range(n_pages)] * 2 + [
                row_blk, row_blk, row_blk,
                pl.BlockSpec((None, 1, sel.shape[2]), per)],
            out_specs=row_blk),
        out_shape=jax.ShapeDtypeStruct((nb, 1, W_A), F32),
        compiler_params=_cparams(("arbitrary",)),
        name="dsa_sample_attn")(page_table, *([pool_k_t] * n_pages), *([pool_v_t] * n_pages),
                                q, k_new, v_new, sel)


def _ret_sample_kernel(q_ref, k_ref, v_ref, g_ref, s_ref, gret_ref, gam_ref, o_ref, sout_ref):
    v = v_ref[...]
    gam = gam_ref[...]

    def body(d, o):
        s_new = s_ref[d] * gam + k_ref[pl.ds(d, 1), :] * v
        sout_ref[d] = s_new
        return o + q_ref[pl.ds(d, 1), :] * s_new

    o = lax.fori_loop(0, HEAD_DIM, body, jnp.zeros(v.shape, F32))
    mu = jnp.mean(o, axis=0, keepdims=True)
    var = jnp.mean(jnp.square(o - mu), axis=0, keepdims=True)
    on = (o - mu) * lax.rsqrt(var + EPS)
    o_ref[...] = on * gret_ref[...] * _swish(g_ref[...])


def _ret_sample_call(qb, kb, vb, gb, state_t, layer, g_ret):
    nb = qb.shape[0]
    gam = jnp.exp(1.0 * _log_gamma())
    gam_b = jnp.broadcast_to(jnp.repeat(gam, HEAD_DIM)[:, None], (W_B, nb))
    gret_b = jnp.broadcast_to(g_ret[:, None], (W_B, nb))
    head = pl.BlockSpec((HEAD_DIM, nb), lambda h: (h, 0))
    o_t, s_new = pl.pallas_call(
        _ret_sample_kernel, grid=(H_B,),
        in_specs=[head, head, head, head,
                  pl.BlockSpec((None, None, HEAD_DIM, HEAD_DIM, nb), lambda h: (layer, h, 0, 0, 0)),
                  head, head],
        out_specs=(head, pl.BlockSpec((None, HEAD_DIM, HEAD_DIM, nb), lambda h: (h, 0, 0, 0))),
        out_shape=(jax.ShapeDtypeStruct((W_B, nb), F32),
                   jax.ShapeDtypeStruct((H_B, HEAD_DIM, HEAD_DIM, nb), F32)),
        compiler_params=_cparams(("arbitrary",)),
        name="ret_sample")(qb.T, kb.T, vb.T, gb.T, state_t, gret_b, gam_b)
    return o_t.T.astype(BF16), s_new


def _split_w_in_t(w_t):
    d = w_t.shape[1]
    starts = np.cumsum((0,) + COL_SIZES).tolist()
    qi = w_t[starts[3]:starts[4]]
    kiwi = w_t[starts[4]:starts[6]]
    qi_pad = jnp.pad(qi.reshape(H_IDX, D_IDX, d), ((0, 0), (0, LANES - D_IDX), (0, 0))).reshape(-1, d)
    w_mid = jnp.concatenate([qi_pad, kiwi, jnp.zeros((LANES - D_IDX - H_IDX, d), w_t.dtype)], axis=0)
    return w_mid, w_t[starts[6]:]


def _rope_tables(pos):
    half = HEAD_DIM // 2
    inv = ROPE_THETA ** (-jnp.arange(half, dtype=F32) / half)
    ang = pos[:, None] * inv[None, :]
    c = jnp.cos(ang)
    s = jnp.sin(ang)
    return jnp.concatenate([c, c, c, c], axis=1), jnp.concatenate([-s, s, -s, s], axis=1)


def kernel(x_prompt, x_sample, cache_k, cache_v, cache_kidx, state_ret, page_table, w_in, w_out,
           g_ret, g_mix_pre, g_mix_post, g_ffn_pre, g_ffn_post, w_up, w_down):
    batch, seq, _ = x_prompt.shape
    nb, dec_seq, _ = x_sample.shape
    depth = w_in.shape[0]
    n_pages = page_table.shape[1]
    past = n_pages * PAGE_SIZE
    assert dec_seq == 1 and seq % DSA_CH == 0 and nb == LANES
    tm_p = DSA_CH

    cos_p, sin_p = _rope_tables(jnp.arange(seq, dtype=F32))
    cos_s, sin_s = _rope_tables(past + jnp.arange(dec_seq, dtype=F32))
    cos_s = jnp.broadcast_to(cos_s, (nb, LANES))
    sin_s = jnp.broadcast_to(sin_s, (nb, LANES))
    pool_k_t = jnp.transpose(cache_k, (0, 1, 3, 4, 2))
    pool_v_t = jnp.transpose(cache_v, (0, 1, 3, 4, 2))
    pool_ki_t = jnp.swapaxes(cache_kidx, 2, 3)
    state_t = jnp.transpose(state_ret, (0, 2, 3, 4, 1))
    w_in_t = jnp.transpose(w_in, (2, 0, 1))
    wo_bf = w_out.astype(BF16)
    wup_bf = w_up.astype(BF16)
    wdn_bf = w_down.astype(BF16)
    topk_s = min(TOPK_MAX, (past + dec_seq) // 4)

    yp = x_prompt.reshape(batch * seq, D_MODEL)
    ys = x_sample.reshape(nb, D_MODEL)
    kp, vp, kip, sp, kss, vss, kis, ss = [], [], [], [], [], [], [], []
    for l in range(depth):
        w_t = w_in_t[:, l, :].astype(BF16)
        w_mid, w_tail = _split_w_in_t(w_t)
        gpre = g_mix_pre[l].reshape(1, D_MODEL)

        (qa, ka, ka_bf, va, va_t, qi, kiwi, kiwi_bf, ki, qb, kb, vb, gb) = _proj_call(
            yp, gpre, w_t, w_mid, w_tail, cos_p, sin_p, tm_p)
        a = _dsa_prompt_call(qa, qi, kiwi, ka_bf, va_t, kiwi_bf, batch, seq)
        bmix, s_new = _ret_prompt_call(qb, kb, vb, gb, g_ret[l], batch, seq)
        yp = _outffn_call(a, bmix, yp, l, wo_bf, g_mix_post[l], g_ffn_pre[l], g_ffn_post[l],
                          wup_bf, wdn_bf, 512)
        kp.append(ka); vp.append(va); kip.append(ki); sp.append(s_new)

        (qa, ka, ka_bf, va, va_t, qi, kiwi, kiwi_bf, ki, qb, kb, vb, gb) = _proj_call(
            ys, gpre, w_t, w_mid, w_tail, cos_s, sin_s, nb)
        qi16 = jnp.pad(qi.reshape(nb, H_IDX, LANES)[:, :, :D_IDX], ((0, 0), (0, 16 - H_IDX), (0, 0)))
        wi16 = jnp.pad(kiwi[:, D_IDX:D_IDX + H_IDX], ((0, 0), (0, 16 - H_IDX)))[:, :, None]
        sc = _dsa_sample_scores_call(page_table, pool_ki_t, l, qi16, wi16, ki.reshape(nb, 1, D_IDX))
        sel = _dsa_sample_select_call(sc.reshape(nb, -1), topk_s, past + dec_seq)
        a = _dsa_sample_attn_call(page_table, pool_k_t, pool_v_t, l,
                                  qa.astype(F32).reshape(nb, 1, W_A),
                                  ka.reshape(nb, 1, W_A), va.reshape(nb, 1, W_A),
                                  sel.reshape(nb, 1, -1))
        bmix, s_new = _ret_sample_call(qb, kb, vb, gb, state_t, l, g_ret[l])
        ys = _outffn_call(a.reshape(nb, W_A).astype(BF16), bmix, ys, l, wo_bf, g_mix_post[l],
                          g_ffn_pre[l], g_ffn_post[l], wup_bf, wdn_bf, nb)
        kss.append(ka); vss.append(va); kis.append(ki); ss.append(s_new)

    def heads(xs, lead):
        return jnp.stack(xs).reshape((depth,) + lead + (H_A, HEAD_DIM))

    ret_s = jnp.transpose(jnp.stack(ss), (0, 4, 1, 2, 3))
    return (yp.reshape(batch, seq, D_MODEL), ys.reshape(nb, dec_seq, D_MODEL),
            heads(kp, (batch, seq)), heads(vp, (batch, seq)),
            jnp.stack(kip).reshape(depth, batch, seq, D_IDX), jnp.stack(sp),
            heads(kss, (nb, dec_seq)), heads(vss, (nb, dec_seq)),
            jnp.stack(kis).reshape(depth, nb, dec_seq, D_IDX), ret_s)
```
